```python
import jax, jax.numpy as jnp
from jax import lax
import numpy as np

D_MODEL = 1024
BATCH = 8
SEQ = 2048
DEPTH = 4

HEAD_DIM = 64
N_HEADS = D_MODEL // HEAD_DIM
N_SB_HEADS = N_HEADS // 2
N_DIL_HEADS = N_HEADS - N_SB_HEADS
N_FOX_HEADS = N_HEADS
D_ATTN = N_HEADS * HEAD_DIM
D_FF = -(-8 * D_MODEL // (3 * 256)) * 256
ROPE_THETA = 500000.0
ROT_DIM = HEAD_DIM // 4
Q_BLOCK = 128
DIL_PATTERNS = ((128, 1), (512, 4), (2048, 16))
RMS_EPS = 1e-5
N_EVEN = (DEPTH + 1) // 2
N_ODD = DEPTH // 2

kernel_name = "hybrid_stickbreak_dilated_fox_trunk"


def rms_norm(x, g):
    xf = x.astype(jnp.float32)
    y = xf * lax.rsqrt(jnp.mean(xf * xf, axis=-1, keepdims=True) + RMS_EPS)
    return (y * g.astype(jnp.float32)).astype(x.dtype)


def partial_rotary(x, pos):
    half = ROT_DIM // 2
    inv_freq = ROPE_THETA ** (-jnp.arange(half, dtype=jnp.float32) * 2.0 / ROT_DIM)
    ang = pos[:, None] * inv_freq[None, :]
    cos = jnp.cos(ang)[:, None, :].astype(x.dtype)
    sin = jnp.sin(ang)[:, None, :].astype(x.dtype)
    x1, x2, x_pass = x[..., :half], x[..., half:ROT_DIM], x[..., ROT_DIM:]
    return jnp.concatenate([x1 * cos - x2 * sin, x2 * cos + x1 * sin, x_pass], axis=-1)


def stick_breaking_attention(q, k, v):
    B, H, S, Dh = q.shape
    nb = S // Q_BLOCK
    qb = q.reshape(B, H, nb, Q_BLOCK, Dh).transpose(2, 0, 1, 3, 4)
    kpos = jnp.arange(S)

    def block(args):
        qblk, n = args
        z = jnp.einsum('bhqd,bhkd->bhqk', qblk, k).astype(jnp.float32)
        qpos = n * Q_BLOCK + jnp.arange(Q_BLOCK)
        strict = kpos[None, :] < qpos[:, None]
        log_1m_beta = jnp.where(strict, jax.nn.log_sigmoid(-z), 0.0)
        between = lax.cumsum(log_1m_beta, axis=3, reverse=True) - log_1m_beta
        a = jnp.where(strict, jnp.exp(jax.nn.log_sigmoid(z) + between), 0.0)
        return jnp.einsum('bhqk,bhkd->bhqd', a.astype(v.dtype), v)

    out = lax.map(block, (qb, jnp.arange(nb)))
    return out.transpose(1, 2, 0, 3, 4).reshape(B, H, S, Dh)


def dilated_window_attention(q, k, v, window, dilation):
    B, S, H, Dh = q.shape
    span = window // dilation
    L = S // dilation
    nb = -(-L // span)
    Lp = nb * span

    def to_blocks(t):
        t = t.reshape(B, L, dilation, H, Dh)
        t = jnp.pad(t, ((0, 0), (0, Lp - L), (0, 0), (0, 0), (0, 0)))
        return t.reshape(B, nb, span, dilation, H, Dh)

    def with_prev(t):
        prev = jnp.pad(t[:, :-1], ((0, 0), (1, 0), (0, 0), (0, 0), (0, 0), (0, 0)))
        return jnp.concatenate([prev, t], axis=2)

    qb = to_blocks(q)
    kw = with_prev(to_blocks(k))
    vw = with_prev(to_blocks(v))
    s = jnp.einsum('bnqrhd,bnkrhd->bnrhqk', qb, kw).astype(jnp.float32)
    a = jnp.arange(span)[None, :, None]
    kk = jnp.arange(2 * span)[None, None, :]
    blk = jnp.arange(nb)[:, None, None]
    valid = (kk >= a) & (kk <= a + span) & (blk * span - span + kk >= 0)
    s = jnp.where(valid[None, :, None, None], s, -jnp.inf)
    m = jnp.max(s, axis=-1, keepdims=True)
    p = jnp.exp(s - m)
    l = jnp.sum(p, axis=-1, keepdims=True)
    o = jnp.einsum('bnrhqk,bnkrhd->bnqrhd', (p / l).astype(v.dtype), vw)
    lse = (m + jnp.log(l))[..., 0]
    o = o.reshape(B, Lp, dilation, H, Dh)[:, :L].reshape(B, S, H, Dh)
    lse = lse.transpose(0, 1, 4, 2, 3).reshape(B, Lp, dilation, H)[:, :L].reshape(B, S, H)
    return o, lse


def forgetting_attention(q, k, v, log_f):
    B, H, S, Dh = q.shape
    F = lax.cumsum(log_f, axis=2)
    nb = S // Q_BLOCK
    qb = q.reshape(B, H, nb, Q_BLOCK, Dh).transpose(2, 0, 1, 3, 4)
    Fb = F.reshape(B, H, nb, Q_BLOCK).transpose(2, 0, 1, 3)
    kpos = jnp.arange(S)

    def block(args):
        qblk, Fq, n = args
        z = jnp.einsum('bhqd,bhkd->bhqk', qblk, k).astype(jnp.float32)
        z = z + Fq[..., None] - F[:, :, None, :]
        qpos = n * Q_BLOCK + jnp.arange(Q_BLOCK)
        z = jnp.where(kpos[None, :] <= qpos[:, None], z, -jnp.inf)
        p = jax.nn.softmax(z, axis=-1)
        return jnp.einsum('bhqk,bhkd->bhqd', p.astype(v.dtype), v)

    out = lax.map(block, (qb, Fb, jnp.arange(nb)))
    return out.transpose(1, 2, 0, 3, 4).reshape(B, H, S, Dh)


def even_mixer(h, w_qkv, w_o):
    B, S, _ = h.shape
    scale = HEAD_DIM ** -0.5
    qkv = (h @ w_qkv).reshape(B, S, 3, N_HEADS, HEAD_DIM)
    q, k, v = qkv[:, :, 0], qkv[:, :, 1], qkv[:, :, 2]
    qa = (q[:, :, :N_SB_HEADS] * scale).transpose(0, 2, 1, 3)
    ka = k[:, :, :N_SB_HEADS].transpose(0, 2, 1, 3)
    va = v[:, :, :N_SB_HEADS].transpose(0, 2, 1, 3)
    out_a = stick_breaking_attention(qa, ka, va).transpose(0, 2, 1, 3)
    pos = jnp.arange(S, dtype=jnp.float32)
    qd = partial_rotary(q[:, :, N_SB_HEADS:], pos) * scale
    kd = partial_rotary(k[:, :, N_SB_HEADS:], pos)
    vd = v[:, :, N_SB_HEADS:]
    outs, lses = [], []
    for window, dilation in DIL_PATTERNS:
        o_p, lse_p = dilated_window_attention(qd, kd, vd, window, dilation)
        outs.append(o_p)
        lses.append(lse_p)
    mix = jax.nn.softmax(jnp.stack(lses), axis=0)
    out_b = jnp.einsum('pbsh,pbshd->bshd', mix.astype(vd.dtype), jnp.stack(outs))
    o = jnp.concatenate([out_a, out_b], axis=2).reshape(B, S, D_ATTN)
    return o @ w_o


def odd_mixer(h, w_qkvf, b_forget, w_o):
    B, S, _ = h.shape
    scale = HEAD_DIM ** -0.5
    proj = h @ w_qkvf
    qkv = proj[..., :3 * D_ATTN].reshape(B, S, 3, N_FOX_HEADS, HEAD_DIM)
    f_logit = proj[..., 3 * D_ATTN:] + b_forget
    log_f = jax.nn.log_sigmoid(f_logit.astype(jnp.float32)).transpose(0, 2, 1)
    q = (qkv[:, :, 0] * scale).transpose(0, 2, 1, 3)
    k = qkv[:, :, 1].transpose(0, 2, 1, 3)
    v = qkv[:, :, 2].transpose(0, 2, 1, 3)
    o = forgetting_attention(q, k, v, log_f).transpose(0, 2, 1, 3).reshape(B, S, D_ATTN)
    return o @ w_o


def swiglu(h, w_in, w_out):
    g, u = jnp.split(h @ w_in, 2, axis=-1)
    return (jax.nn.silu(g) * u) @ w_out


def setup_inputs(seed: int = 0) -> dict:
    key = jax.random.key(seed)
    ks = jax.random.split(key, 12)
    f32 = jnp.float32
    res_scale = (2.0 * DEPTH) ** -0.5
    x = jax.random.normal(ks[0], (BATCH, SEQ, D_MODEL), f32)
    norm_mix = 1.0 + 0.02 * jax.random.normal(ks[1], (DEPTH, D_MODEL), f32)
    w_qkv_even = jax.random.normal(ks[2], (N_EVEN, D_MODEL, 3 * D_ATTN), f32) * D_MODEL ** -0.5
    w_o_even = jax.random.normal(ks[3], (N_EVEN, D_ATTN, D_MODEL), f32) * (D_ATTN ** -0.5 * res_scale)
    w_qkvf_odd = jax.random.normal(ks[4], (N_ODD, D_MODEL, 3 * D_ATTN + N_FOX_HEADS), f32) * D_MODEL ** -0.5
    b_forget = jax.random.uniform(ks[5], (N_ODD, N_FOX_HEADS), f32, minval=1.0, maxval=4.0)
    w_o_odd = jax.random.normal(ks[6], (N_ODD, D_ATTN, D_MODEL), f32) * (D_ATTN ** -0.5 * res_scale)
    norm_ffn = 1.0 + 0.02 * jax.random.normal(ks[7], (DEPTH, D_MODEL), f32)
    w_ffn_in = jax.random.normal(ks[8], (DEPTH, D_MODEL, 2 * D_FF), f32) * D_MODEL ** -0.5
    w_ffn_out = jax.random.normal(ks[9], (DEPTH, D_FF, D_MODEL), f32) * (D_FF ** -0.5 * res_scale)
    norm_final = 1.0 + 0.02 * jax.random.normal(ks[10], (D_MODEL,), f32)
    return {"x": x, "norm_mix": norm_mix, "w_qkv_even": w_qkv_even, "w_o_even": w_o_even,
            "w_qkvf_odd": w_qkvf_odd, "b_forget": b_forget, "w_o_odd": w_o_odd,
            "norm_ffn": norm_ffn, "w_ffn_in": w_ffn_in, "w_ffn_out": w_ffn_out,
            "norm_final": norm_final}


def reference(x, norm_mix, w_qkv_even, w_o_even, w_qkvf_odd, b_forget, w_o_odd,
              norm_ffn, w_ffn_in, w_ffn_out, norm_final):
    for layer in range(DEPTH):
        h = rms_norm(x, norm_mix[layer])
        if layer % 2 == 0:
            x = x + even_mixer(h, w_qkv_even[layer // 2], w_o_even[layer // 2])
        else:
            x = x + odd_mixer(h, w_qkvf_odd[layer // 2], b_forget[layer // 2], w_o_odd[layer // 2])
        h = rms_norm(x, norm_ffn[layer])
        x = x + swiglu(h, w_ffn_in[layer], w_ffn_out[layer])
    return rms_norm(x, norm_final)
```

```python
import functools

import jax
import jax.numpy as jnp
from jax import lax
from jax.experimental import pallas as pl
from jax.experimental.pallas import tpu as pltpu

D_MODEL = 1024
BATCH = 8
SEQ = 2048
DEPTH = 4
HEAD_DIM = 64
N_HEADS = 16
D_ATTN = N_HEADS * HEAD_DIM
D_FF = 2816
ROPE_THETA = 500000.0
ROT_DIM = HEAD_DIM // 4
DIL_PATTERNS = ((128, 1), (512, 4), (2048, 16))
RMS_EPS = 1e-5
SCALE = HEAD_DIM ** -0.5

LANES = 128
M_TOKENS = BATCH * SEQ
TM = 512
TN = 1024
TF = 256
TQ = 256
TK = 256
DB = 128
N_DB = SEQ // DB
VMEM_LIMIT = 56 * 1024 * 1024

BF = jnp.bfloat16
F32 = jnp.float32
NEG_INF = float("-inf")


def _rms(x, g):
    ms = jnp.mean(x * x, axis=-1, keepdims=True)
    return x * lax.rsqrt(ms + RMS_EPS) * g


def _dot(a, b):
    return jnp.dot(a, b, preferred_element_type=F32)


def _dot_nt(a, b):
    return lax.dot_general(a, b, (((1,), (1,)), ((), ())), preferred_element_type=F32)


def _softplus(z):
    return jnp.maximum(z, 0.0) + jnp.log1p(jnp.exp(-jnp.abs(z)))


def _qkv_kernel(x_ref, g_ref, w_ref, o_ref):
    hn = _rms(x_ref[...], g_ref[...]).astype(BF)
    for j in range(3 * D_ATTN // TN):
        o_ref[:, j * TN:(j + 1) * TN] = _dot(hn, w_ref[:, j * TN:(j + 1) * TN]).astype(BF)


def _qkvf_kernel(x_ref, g_ref, w_ref, wf_ref, o_ref, f_ref):
    hn = _rms(x_ref[...], g_ref[...]).astype(BF)
    for j in range(3 * D_ATTN // TN):
        o_ref[:, j * TN:(j + 1) * TN] = _dot(hn, w_ref[:, j * TN:(j + 1) * TN]).astype(BF)
    f_ref[...] = _dot(hn, wf_ref[...])


def _const_spec(shape):
    return pl.BlockSpec(shape, lambda *_: (0,) * len(shape), pipeline_mode=pl.Buffered(1))


def _qkv_call(x, g, w, wf=None):
    n = 3 * D_ATTN
    in_specs = [pl.BlockSpec((TM, D_MODEL), lambda i: (i, 0)),
                _const_spec((1, D_MODEL)),
                _const_spec((D_MODEL, n))]
    out_specs = pl.BlockSpec((TM, n), lambda i: (i, 0))
    out_shape = jax.ShapeDtypeStruct((M_TOKENS, n), BF)
    args = [x, g.reshape(1, D_MODEL), w]
    body = _qkv_kernel
    if wf is not None:
        in_specs.append(_const_spec((D_MODEL, LANES)))
        out_specs = [out_specs, pl.BlockSpec((TM, LANES), lambda i: (i, 0))]
        out_shape = [out_shape, jax.ShapeDtypeStruct((M_TOKENS, LANES), F32)]
        args.append(wf)
        body = _qkvf_kernel
    return pl.pallas_call(
        body, grid=(M_TOKENS // TM,), in_specs=in_specs, out_specs=out_specs, out_shape=out_shape,
        compiler_params=pltpu.CompilerParams(dimension_semantics=("parallel",),
                                             vmem_limit_bytes=VMEM_LIMIT),
        name="norm_qkv" if wf is None else "norm_qkvf")(*args)


def _ffn_kernel(x_ref, oa_ref, ob_ref, wo_ref, g_ref, win_ref, wout_ref, gf_ref, out_ref,
                acc_ref, hn_ref, *, final_norm):
    half = D_ATTN // 2
    xn = x_ref[...] + _dot(oa_ref[...], wo_ref[0:half, :]) + _dot(ob_ref[...], wo_ref[half:, :])
    acc_ref[...] = xn
    hn_ref[...] = _rms(xn, g_ref[...]).astype(BF)

    def body(f, c):
        c0 = pl.multiple_of(f * TF, TF)
        c1 = pl.multiple_of(D_FF + f * TF, TF)
        hn = hn_ref[...]
        g = _dot(hn, win_ref[:, pl.ds(c0, TF)])
        u = _dot(hn, win_ref[:, pl.ds(c1, TF)])
        a = (g * (1.0 / (1.0 + jnp.exp(-g))) * u).astype(BF)
        acc_ref[...] += _dot(a, wout_ref[pl.ds(c0, TF), :])
        return c

    lax.fori_loop(0, D_FF // TF, body, 0)
    y = acc_ref[...]
    if final_norm:
        y = _rms(y, gf_ref[...])
    out_ref[...] = y


def _ffn_call(x, oa, ob, ob_col, wo, g, win, wout, gf, final_norm):
    half = D_ATTN // 2
    in_specs = [pl.BlockSpec((TM, D_MODEL), lambda i: (i, 0)),
                pl.BlockSpec((TM, half), lambda i: (i, 0)),
                pl.BlockSpec((TM, half), lambda i: (i, ob_col)),
                _const_spec((D_ATTN, D_MODEL)),
                _const_spec((1, D_MODEL)),
                _const_spec((D_MODEL, 2 * D_FF)),
                _const_spec((D_FF, D_MODEL)),
                _const_spec((1, D_MODEL))]
    return pl.pallas_call(
        functools.partial(_ffn_kernel, final_norm=final_norm),
        grid=(M_TOKENS // TM,), in_specs=in_specs,
        out_specs=pl.BlockSpec((TM, D_MODEL), lambda i: (i, 0)),
        out_shape=jax.ShapeDtypeStruct((M_TOKENS, D_MODEL), F32),
        scratch_shapes=[pltpu.VMEM((TM, D_MODEL), F32), pltpu.VMEM((TM, D_MODEL), BF)],
        compiler_params=pltpu.CompilerParams(dimension_semantics=("parallel",),
                                             vmem_limit_bytes=VMEM_LIMIT),
        name="oproj_ffn")(x, oa, ob, wo, g.reshape(1, D_MODEL), win, wout, gf.reshape(1, D_MODEL))


def _fprep_kernel(fl_ref, b_ref, o_ref):
    x = fl_ref[0] + b_ref[...]
    lf = jnp.minimum(x, 0.0) - jnp.log1p(jnp.exp(-jnp.abs(x)))
    lft = lf.T
    r = lax.broadcasted_iota(jnp.int32, (TK, TK), 0)
    c = lax.broadcasted_iota(jnp.int32, (TK, TK), 1)
    u = jnp.where(r <= c, 1.0, 0.0).astype(BF)
    carry = jnp.zeros((N_HEADS, 1), F32)
    for cb in range(SEQ // TK):
        blk = lft[0:N_HEADS, cb * TK:(cb + 1) * TK]
        b0 = blk.astype(BF)
        r1 = blk - b0.astype(F32)
        b1 = r1.astype(BF)
        b2 = (r1 - b1.astype(F32)).astype(BF)
        cs = _dot(b0, u) + _dot(b1, u) + _dot(b2, u) + carry
        o_ref[0, :, cb * TK:(cb + 1) * TK] = cs
        carry = cs[:, TK - 1:TK]


def _fprep_call(flog, bias):
    return pl.pallas_call(
        _fprep_kernel, grid=(BATCH,),
        in_specs=[pl.BlockSpec((1, SEQ, LANES), lambda b: (b, 0, 0)),
                  pl.BlockSpec((1, LANES), lambda b: (0, 0))],
        out_specs=pl.BlockSpec((1, N_HEADS, SEQ), lambda b: (b, 0, 0)),
        out_shape=jax.ShapeDtypeStruct((BATCH, N_HEADS, SEQ), F32),
        compiler_params=pltpu.CompilerParams(dimension_semantics=("parallel",),
                                             vmem_limit_bytes=VMEM_LIMIT),
        name="forget_cumsum")(flog, bias)


def _fox_kernel(q_ref, k_ref, v_ref, f_ref, o_ref, va_ref, vb_ref):
    pair = pl.program_id(1)
    qi = pl.program_id(2)

    @pl.when(qi == 0)
    def _():
        lane_kv = lax.broadcasted_iota(jnp.int32, (SEQ, LANES), 1)
        v = v_ref[0]
        one = jnp.ones_like(v)
        va_ref[...] = jnp.where(lane_kv < HEAD_DIM, v, one)
        vb_ref[...] = jnp.where(lane_kv >= HEAD_DIM, v, one)

    lane = lax.broadcasted_iota(jnp.int32, (TQ, LANES), 1)
    row = lax.broadcasted_iota(jnp.int32, (TQ, TK), 0)
    col = lax.broadcasted_iota(jnp.int32, (TQ, TK), 1)
    causal = col <= row
    q = q_ref[0] * jnp.asarray(SCALE, BF)
    zero = jnp.zeros_like(q)
    outs = []
    for hh, vx_ref in ((0, va_ref), (1, vb_ref)):
        qh = jnp.where((lane < HEAD_DIM) if hh == 0 else (lane >= HEAD_DIM), q, zero)
        hrow = 2 * pair + hh

        def blk(j, m, acc, masked, qh=qh, vx_ref=vx_ref, hrow=hrow):
            k0 = pl.multiple_of(j * TK, TK)
            k = k_ref[0, pl.ds(k0, TK), :]
            v = vx_ref[pl.ds(k0, TK), :]
            fk = f_ref[0, pl.ds(hrow, 1), pl.ds(k0, TK)]
            s = _dot_nt(qh, k) - fk
            if masked:
                s = jnp.where(causal, s, NEG_INF)
            m_new = jnp.maximum(m, jnp.max(s, axis=-1, keepdims=True))
            alpha = jnp.exp(m - m_new)
            pm = jnp.exp(s - m_new)
            acc = alpha * acc + _dot(pm.astype(BF), v)
            return m_new, acc

        m0 = jnp.full((TQ, 1), NEG_INF, F32)
        acc0 = jnp.zeros((TQ, LANES), F32)
        m, acc = lax.fori_loop(0, qi, lambda j, c: blk(j, c[0], c[1], False), (m0, acc0))
        m, acc = blk(qi, m, acc, True)
        outs.append(acc / pltpu.roll(acc, HEAD_DIM, 1))
    o_ref[0] = jnp.where(lane < HEAD_DIM, outs[0], outs[1]).astype(BF)


def _fox_call(qkv3, f_row):
    n_pairs = N_HEADS // 2
    return pl.pallas_call(
        _fox_kernel, grid=(BATCH, n_pairs, SEQ // TQ),
        in_specs=[pl.BlockSpec((1, TQ, LANES), lambda b, p, i: (b, i, p)),
                  pl.BlockSpec((1, SEQ, LANES), lambda b, p, i: (b, 0, n_pairs + p)),
                  pl.BlockSpec((1, SEQ, LANES), lambda b, p, i: (b, 0, 2 * n_pairs + p)),
                  pl.BlockSpec((1, N_HEADS, SEQ), lambda b, p, i: (b, 0, 0))],
        out_specs=pl.BlockSpec((1, TQ, LANES), lambda b, p, i: (b, i, p)),
        out_shape=jax.ShapeDtypeStruct((BATCH, SEQ, D_ATTN), BF),
        scratch_shapes=[pltpu.VMEM((SEQ, LANES), BF), pltpu.VMEM((SEQ, LANES), BF)],
        compiler_params=pltpu.CompilerParams(
            dimension_semantics=("parallel", "arbitrary", "arbitrary"),
            vmem_limit_bytes=VMEM_LIMIT),
        name="fox_attn")(qkv3, qkv3, qkv3, f_row)


def _sb_kernel(q_ref, k_ref, v_ref, o_ref, u_ref):
    qi = pl.program_id(2)

    @pl.when(qi == 0)
    def _():
        r = lax.broadcasted_iota(jnp.int32, (TK, TK), 0)
        c = lax.broadcasted_iota(jnp.int32, (TK, TK), 1)
        u_ref[...] = jnp.where(r > c, 1.0, 0.0).astype(BF)

    lane = lax.broadcasted_iota(jnp.int32, (TQ, LANES), 1)
    row = lax.broadcasted_iota(jnp.int32, (TQ, TK), 0)
    col = lax.broadcasted_iota(jnp.int32, (TQ, TK), 1)
    strict = col < row
    q = q_ref[0] * jnp.asarray(SCALE, BF)
    zero = jnp.zeros_like(q)
    outs = []
    for hh in (0, 1):
        qh = jnp.where((lane < HEAD_DIM) if hh == 0 else (lane >= HEAD_DIM), q, zero)

        def blk(j, carry, acc, masked, qh=qh):
            k0 = pl.multiple_of(j * TK, TK)
            k = k_ref[0, pl.ds(k0, TK), :]
            v = v_ref[0, pl.ds(k0, TK), :]
            z = _dot_nt(qh, k)
            sp = _softplus(z)
            lg = -sp
            if masked:
                lg = jnp.where(strict, lg, 0.0)
            lh = lg.astype(BF)
            ll = (lg - lh.astype(F32)).astype(BF)
            u = u_ref[...]
            suf = _dot(lh, u) + _dot(ll, u)
            a = jnp.exp((z - sp) + (suf + carry))
            if masked:
                a = jnp.where(strict, a, 0.0)
            acc = acc + _dot(a.astype(BF), v)
            carry = carry + jnp.sum(lg, axis=-1, keepdims=True)
            return carry, acc

        c0 = jnp.zeros((TQ, 1), F32)
        acc0 = jnp.zeros((TQ, LANES), F32)
        c, acc = blk(qi, c0, acc0, True)
        c, acc = lax.fori_loop(0, qi, lambda t, cc: blk(qi - 1 - t, cc[0], cc[1], False), (c, acc))
        outs.append(acc)
    o_ref[0] = jnp.where(lane < HEAD_DIM, outs[0], outs[1]).astype(BF)


def _sb_call(qkv3):
    n_pairs = N_HEADS // 2
    n_sb = n_pairs // 2
    return pl.pallas_call(
        _sb_kernel, grid=(BATCH, n_sb, SEQ // TQ),
        in_specs=[pl.BlockSpec((1, TQ, LANES), lambda b, p, i: (b, i, p)),
                  pl.BlockSpec((1, SEQ, LANES), lambda b, p, i: (b, 0, n_pairs + p)),
                  pl.BlockSpec((1, SEQ, LANES), lambda b, p, i: (b, 0, 2 * n_pairs + p))],
        out_specs=pl.BlockSpec((1, TQ, LANES), lambda b, p, i: (b, i, p)),
        out_shape=jax.ShapeDtypeStruct((BATCH, SEQ, D_ATTN // 2), BF),
        scratch_shapes=[pltpu.VMEM((TK, TK), BF)],
        compiler_params=pltpu.CompilerParams(
            dimension_semantics=("parallel", "arbitrary", "arbitrary"),
            vmem_limit_bytes=VMEM_LIMIT),
        name="stickbreak_attn")(qkv3, qkv3, qkv3)


def _dil_kernel(q_ref, k_ref, v_ref, c_ref, s1_ref, s2_ref, o_ref,
                nat_ref, qp_ref, kp_ref, vpa_ref, vpb_ref, x_ref, on_ref):
    n_pat = len(DIL_PATTERNS)
    lane = lax.broadcasted_iota(jnp.int32, (DB, LANES), 1)
    is_a = lane < HEAD_DIM

    half = ROT_DIM // 2

    def rot(x, rows):
        return (x * c_ref[rows, :] + pltpu.roll(x, LANES - half, 1) * s1_ref[rows, :]
                + pltpu.roll(x, half, 1) * s2_ref[rows, :])

    for t in range(N_DB):
        rows = slice(t * DB, (t + 1) * DB)
        nat_ref[0, rows, :] = rot(q_ref[0, rows, :].astype(F32), rows) * SCALE
        nat_ref[1, rows, :] = rot(k_ref[0, rows, :].astype(F32), rows)
        nat_ref[2, rows, :] = v_ref[0, rows, :].astype(F32)

    for pi, (window, d) in enumerate(DIL_PATTERNS):
        per_res = SEQ // d // DB
        for t in range(N_DB):
            r, c = divmod(t, per_res)
            src = pl.ds(r + d * DB * c, DB, stride=d) if d > 1 else pl.ds(t * DB, DB)
            dst = slice(t * DB, (t + 1) * DB)
            qp_ref[pi, dst, :] = nat_ref[0, src, :].astype(BF)
            kp_ref[pi, dst, :] = nat_ref[1, src, :].astype(BF)
            v = nat_ref[2, src, :]
            vpa_ref[pi, dst, :] = jnp.where(is_a, v, 1.0).astype(BF)
            vpb_ref[pi, dst, :] = jnp.where(is_a, 1.0, v).astype(BF)

    row = lax.broadcasted_iota(jnp.int32, (DB, DB), 0)
    col = lax.broadcasted_iota(jnp.int32, (DB, DB), 1)
    cur_ok = col <= row

    for pi, (window, d) in enumerate(DIL_PATTERNS):
        per_res = SEQ // d // DB
        use_prev = per_res > 1
        for hh, vx_ref in ((0, vpa_ref), (1, vpb_ref)):
            mine = is_a if hh == 0 else jnp.logical_not(is_a)

            def body(t, carry, pi=pi, hh=hh, vx_ref=vx_ref, mine=mine, per_res=per_res,
                     use_prev=use_prev):
                r0 = pl.multiple_of(t * DB, DB)
                q = qp_ref[pi, pl.ds(r0, DB), :]
                q = jnp.where(mine, q, jnp.zeros_like(q))
                kc = kp_ref[pi, pl.ds(r0, DB), :]
                vc = vx_ref[pi, pl.ds(r0, DB), :]
                s_cur = jnp.where(cur_ok, _dot_nt(q, kc), NEG_INF)
                if use_prev:
                    p0 = pl.multiple_of(jnp.maximum(t - 1, 0) * DB, DB)
                    off = jnp.where((t % per_res) != 0, 0, DB)
                    kp = kp_ref[pi, pl.ds(p0, DB), :]
                    vp = vx_ref[pi, pl.ds(p0, DB), :]
                    s_prev = jnp.where(col >= row + off, _dot_nt(q, kp), NEG_INF)
                    m = jnp.maximum(jnp.max(s_cur, axis=-1, keepdims=True),
                                    jnp.max(s_prev, axis=-1, keepdims=True))
                    acc = (_dot(jnp.exp(s_cur - m).astype(BF), vc)
                           + _dot(jnp.exp(s_prev - m).astype(BF), vp))
                else:
                    m = jnp.max(s_cur, axis=-1, keepdims=True)
                    acc = _dot(jnp.exp(s_cur - m).astype(BF), vc)
                den = pltpu.roll(acc, HEAD_DIM, 1)
                lse = m + jnp.log(jnp.maximum(acc, 1e-30))
                x_ref[pi, hh, pl.ds(r0, DB), :] = jnp.where(mine, acc / den, lse)
                return carry

            lax.fori_loop(0, N_DB, body, 0)

    d_max = DIL_PATTERNS[-1][1]
    for r in range(d_max):
        res = []
        for hh in (0, 1):
            xs = []
            for pi, (window, d) in enumerate(DIL_PATTERNS):
                step = d_max // d
                start = (r % d) * (SEQ // d) + r // d
                rows = pl.ds(start, DB, stride=step) if step > 1 else pl.ds(start, DB)
                xs.append(x_ref[pi, hh, rows, :])
            ls = [pltpu.roll(x, HEAD_DIM, 1) for x in xs]
            mx = functools.reduce(jnp.maximum, ls)
            es = [jnp.exp(l - mx) for l in ls]
            tot = functools.reduce(lambda a, b: a + b, es)
            mix = [e / tot for e in es]
            res.append(functools.reduce(lambda a, b: a + b, [w * x for w, x in zip(mix, xs)]))
        on_ref[pl.ds(r, DB, stride=d_max), :] = jnp.where(is_a, res[0], res[1])

    o_ref[0] = on_ref[...].astype(BF)


def _dil_call(qkv3, cos_t, sin1_t, sin2_t):
    n_pairs = N_HEADS // 2
    n_sb = n_pairs // 2
    n_pat = len(DIL_PATTERNS)
    tab = pl.BlockSpec((SEQ, LANES), lambda b, p: (0, 0))
    return pl.pallas_call(
        _dil_kernel, grid=(BATCH, n_pairs - n_sb),
        in_specs=[pl.BlockSpec((1, SEQ, LANES), lambda b, p: (b, 0, n_sb + p)),
                  pl.BlockSpec((1, SEQ, LANES), lambda b, p: (b, 0, n_pairs + n_sb + p)),
                  pl.BlockSpec((1, SEQ, LANES), lambda b, p: (b, 0, 2 * n_pairs + n_sb + p)),
                  tab, tab, tab],
        out_specs=pl.BlockSpec((1, SEQ, LANES), lambda b, p: (b, 0, p)),
        out_shape=jax.ShapeDtypeStruct((BATCH, SEQ, D_ATTN // 2), BF),
        scratch_shapes=[pltpu.VMEM((3, SEQ, LANES), F32),
                        pltpu.VMEM((n_pat, SEQ, LANES), BF),
                        pltpu.VMEM((n_pat, SEQ, LANES), BF),
                        pltpu.VMEM((n_pat, SEQ, LANES), BF),
                        pltpu.VMEM((n_pat, SEQ, LANES), BF),
                        pltpu.VMEM((n_pat, 2, SEQ, LANES), F32),
                        pltpu.VMEM((SEQ, LANES), F32)],
        compiler_params=pltpu.CompilerParams(dimension_semantics=("parallel", "arbitrary"),
                                             vmem_limit_bytes=VMEM_LIMIT),
        name="dilated_attn")(qkv3, qkv3, qkv3, cos_t, sin1_t, sin2_t)


def _rotary_tables():
    half = ROT_DIM // 2
    pos = jnp.arange(SEQ, dtype=F32)
    inv_freq = ROPE_THETA ** (-jnp.arange(half, dtype=F32) * 2.0 / ROT_DIM)
    ang = pos[:, None] * inv_freq[None, :]
    cos, sin = jnp.cos(ang), jnp.sin(ang)
    zeros = jnp.zeros((SEQ, HEAD_DIM - ROT_DIM), F32)
    z8 = jnp.zeros((SEQ, half), F32)
    c_head = jnp.concatenate([cos, cos, zeros + 1.0], axis=1)
    s1_head = jnp.concatenate([-sin, z8, zeros], axis=1)
    s2_head = jnp.concatenate([z8, sin, zeros], axis=1)
    two = lambda t: jnp.concatenate([t, t], axis=1)
    return two(c_head), two(s1_head), two(s2_head)


def kernel(x, norm_mix, w_qkv_even, w_o_even, w_qkvf_odd, b_forget, w_o_odd, norm_ffn,
           w_ffn_in, w_ffn_out, norm_final):
    h = x.reshape(M_TOKENS, D_MODEL)
    cos_t, sin1_t, sin2_t = _rotary_tables()
    n_qkv = 3 * D_ATTN
    for layer in range(DEPTH):
        i = layer // 2
        if layer % 2 == 0:
            qkv = _qkv_call(h, norm_mix[layer], w_qkv_even[i].astype(BF))
            qkv3 = qkv.reshape(BATCH, SEQ, n_qkv)
            oa = _sb_call(qkv3).reshape(M_TOKENS, D_ATTN // 2)
            ob = _dil_call(qkv3, cos_t, sin1_t, sin2_t).reshape(M_TOKENS, D_ATTN // 2)
            ob_col = 0
            wo = w_o_even[i]
        else:
            w = w_qkvf_odd[i]
            wf = jnp.pad(w[:, n_qkv:], ((0, 0), (0, LANES - N_HEADS))).astype(BF)
            qkv, flog = _qkv_call(h, norm_mix[layer], w[:, :n_qkv].astype(BF), wf)
            bias = jnp.pad(b_forget[i], (0, LANES - N_HEADS)).reshape(1, LANES)
            f_row = _fprep_call(flog.reshape(BATCH, SEQ, LANES), bias)
            o = _fox_call(qkv.reshape(BATCH, SEQ, n_qkv), f_row).reshape(M_TOKENS, D_ATTN)
            oa, ob, ob_col = o, o, 1
            wo = w_o_odd[i]
        h = _ffn_call(h, oa, ob, ob_col, wo.astype(BF), norm_ffn[layer],
                      w_ffn_in[layer].astype(BF), w_ffn_out[layer].astype(BF), norm_final,
                      final_norm=(layer == DEPTH - 1))
    return h.reshape(BATCH, SEQ, D_MODEL)
```

```python
import functools
import math

import jax
import jax.numpy as jnp
from jax import lax
from jax.experimental import pallas as pl
from jax.experimental.pallas import tpu as pltpu

D_MODEL = 1024
BATCH = 8
SEQ = 2048
DEPTH = 4
HEAD_DIM = 64
N_HEADS = 16
D_ATTN = N_HEADS * HEAD_DIM
D_FF = 2816
ROPE_THETA = 500000.0
ROT_DIM = HEAD_DIM // 4
DIL_PATTERNS = ((128, 1), (512, 4), (2048, 16))
RMS_EPS = 1e-5
LOG2E = math.log2(math.e)
Q_SCALE = HEAD_DIM ** -0.5 * LOG2E

LANES = 128
M_TOKENS = BATCH * SEQ
TM = 512
TN = 1024
TF = 256
TQ = 1024
TK = 256
DB = 128
N_DB = SEQ // DB
DIL_UNROLL = 2
VMEM_LIMIT = 56 * 1024 * 1024

BF = jnp.bfloat16
F32 = jnp.float32
NEG_INF = float("-inf")


def _rms(x, g):
    ms = jnp.mean(x * x, axis=-1, keepdims=True)
    return x * lax.rsqrt(ms + RMS_EPS) * g


def _dot(a, b):
    return jnp.dot(a, b, preferred_element_type=F32)


def _dot_nt(a, b):
    return lax.dot_general(a, b, (((1,), (1,)), ((), ())), preferred_element_type=F32)


def _set_rows(full, r0, part):
    return part if r0 == 0 else jnp.concatenate([full[:r0], part], axis=0)


def _qkv_kernel(x_ref, g_ref, w_ref, *rest):
    hn = _rms(x_ref[...], g_ref[...]).astype(BF)
    o_ref = rest[-1] if len(rest) == 1 else rest[1]
    for j in range(3 * D_ATTN // TN):
        y = _dot(hn, w_ref[:, j * TN:(j + 1) * TN])
        if (j + 1) * TN <= D_ATTN:
            y = y * Q_SCALE
        o_ref[:, j * TN:(j + 1) * TN] = y.astype(BF)
    if len(rest) == 3:
        wf_ref, _, f_ref = rest
        f_ref[...] = _dot(hn, wf_ref[...])


def _const_spec(shape):
    return pl.BlockSpec(shape, lambda *_: (0,) * len(shape), pipeline_mode=pl.Buffered(1))


def _qkv_call(x, g, w, wf=None):
    n = 3 * D_ATTN
    in_specs = [pl.BlockSpec((TM, D_MODEL), lambda i: (i, 0)),
                _const_spec((1, D_MODEL)),
                _const_spec((D_MODEL, n))]
    out_specs = pl.BlockSpec((TM, n), lambda i: (i, 0))
    out_shape = jax.ShapeDtypeStruct((M_TOKENS, n), BF)
    args = [x, g.reshape(1, D_MODEL), w]
    if wf is not None:
        in_specs.append(_const_spec((D_MODEL, LANES)))
        out_specs = [out_specs, pl.BlockSpec((TM, LANES), lambda i: (i, 0))]
        out_shape = [out_shape, jax.ShapeDtypeStruct((M_TOKENS, LANES), F32)]
        args.append(wf)
    return pl.pallas_call(
        _qkv_kernel, grid=(M_TOKENS // TM,), in_specs=in_specs, out_specs=out_specs,
        out_shape=out_shape,
        compiler_params=pltpu.CompilerParams(dimension_semantics=("parallel",),
                                             vmem_limit_bytes=VMEM_LIMIT),
        name="norm_qkv" if wf is None else "norm_qkvf")(*args)


def _ffn_kernel(x_ref, oa_ref, ob_ref, wo_ref, g_ref, win_ref, wout_ref, gf_ref, out_ref,
                acc_ref, hn_ref, *, final_norm):
    half = D_ATTN // 2
    xn = x_ref[...] + _dot(oa_ref[...], wo_ref[0:half, :]) + _dot(ob_ref[...], wo_ref[half:, :])
    acc_ref[...] = xn
    hn_ref[...] = _rms(xn, g_ref[...]).astype(BF)

    def body(f, c):
        c0 = pl.multiple_of(f * TF, TF)
        c1 = pl.multiple_of(D_FF + f * TF, TF)
        hn = hn_ref[...]
        g = _dot(hn, win_ref[:, pl.ds(c0, TF)])
        u = _dot(hn, win_ref[:, pl.ds(c1, TF)])
        a = (g * (1.0 / (1.0 + jnp.exp(-g))) * u).astype(BF)
        acc_ref[...] += _dot(a, wout_ref[pl.ds(c0, TF), :])
        return c

    lax.fori_loop(0, D_FF // TF, body, 0)
    y = acc_ref[...]
    if final_norm:
        y = _rms(y, gf_ref[...])
    out_ref[...] = y


def _ffn_call(x, oa, ob, ob_col, wo, g, win, wout, gf, final_norm):
    half = D_ATTN // 2
    in_specs = [pl.BlockSpec((TM, D_MODEL), lambda i: (i, 0)),
                pl.BlockSpec((TM, half), lambda i: (i, 0)),
                pl.BlockSpec((TM, half), lambda i: (i, ob_col)),
                _const_spec((D_ATTN, D_MODEL)),
                _const_spec((1, D_MODEL)),
                _const_spec((D_MODEL, 2 * D_FF)),
                _const_spec((D_FF, D_MODEL)),
                _const_spec((1, D_MODEL))]
    return pl.pallas_call(
        functools.partial(_ffn_kernel, final_norm=final_norm),
        grid=(M_TOKENS // TM,), in_specs=in_specs,
        out_specs=pl.BlockSpec((TM, D_MODEL), lambda i: (i, 0)),
        out_shape=jax.ShapeDtypeStruct((M_TOKENS, D_MODEL), F32),
        scratch_shapes=[pltpu.VMEM((TM, D_MODEL), F32), pltpu.VMEM((TM, D_MODEL), BF)],
        compiler_params=pltpu.CompilerParams(dimension_semantics=("parallel",),
                                             vmem_limit_bytes=VMEM_LIMIT),
        name="oproj_ffn")(x, oa, ob, wo, g.reshape(1, D_MODEL), win, wout, gf.reshape(1, D_MODEL))


def _fprep_kernel(fl_ref, b_ref, o_ref):
    x = fl_ref[0] + b_ref[...]
    lf = jnp.minimum(x, 0.0) - jnp.log1p(jnp.exp(-jnp.abs(x)))
    lft = lf.T
    r = lax.broadcasted_iota(jnp.int32, (TK, TK), 0)
    c = lax.broadcasted_iota(jnp.int32, (TK, TK), 1)
    u = jnp.where(r <= c, 1.0, 0.0).astype(BF)
    carry = jnp.zeros((N_HEADS, 1), F32)
    for cb in range(SEQ // TK):
        blk = lft[0:N_HEADS, cb * TK:(cb + 1) * TK]
        b0 = blk.astype(BF)
        r1 = blk - b0.astype(F32)
        b1 = r1.astype(BF)
        b2 = (r1 - b1.astype(F32)).astype(BF)
        cs = _dot(b0, u) + _dot(b1, u) + _dot(b2, u) + carry
        o_ref[0, :, cb * TK:(cb + 1) * TK] = cs * LOG2E
        carry = cs[:, TK - 1:TK]


def _fprep_call(flog, bias):
    return pl.pallas_call(
        _fprep_kernel, grid=(BATCH,),
        in_specs=[pl.BlockSpec((1, SEQ, LANES), lambda b: (b, 0, 0)),
                  pl.BlockSpec((1, LANES), lambda b: (0, 0))],
        out_specs=pl.BlockSpec((1, N_HEADS, SEQ), lambda b: (b, 0, 0)),
        out_shape=jax.ShapeDtypeStruct((BATCH, N_HEADS, SEQ), F32),
        compiler_params=pltpu.CompilerParams(dimension_semantics=("parallel",),
                                             vmem_limit_bytes=VMEM_LIMIT),
        name="forget_cumsum")(flog, bias)


def _tri_mask(rows, strict):
    row = lax.broadcasted_iota(jnp.int32, (rows, TK), 0)
    col = lax.broadcasted_iota(jnp.int32, (rows, TK), 1)
    return (col < row) if strict else (col <= row)


def _head_split(q):
    lane = lax.broadcasted_iota(jnp.int32, q.shape, 1)
    zero = jnp.zeros_like(q)
    return jnp.where(lane < HEAD_DIM, q, zero), jnp.where(lane >= HEAD_DIM, q, zero)


def _fox_kernel(q_ref, k_ref, v_ref, f_ref, o_ref, va_ref, vb_ref, s_ref):
    pair = pl.program_id(1)
    qi = pl.program_id(2)
    n_diag = TQ // TK
    n_full = qi * n_diag

    @pl.when(qi == 0)
    def _():
        lane_kv = lax.broadcasted_iota(jnp.int32, (SEQ, LANES), 1)
        v = v_ref[0]
        one = jnp.ones_like(v)
        va_ref[...] = jnp.where(lane_kv < HEAD_DIM, v, one)
        vb_ref[...] = jnp.where(lane_kv >= HEAD_DIM, v, one)

    qhs = _head_split(q_ref[0])
    vxs = (va_ref, vb_ref)

    def scores(j, mx, r0, mask):
        k0 = pl.multiple_of(j * TK, TK)
        k = k_ref[0, pl.ds(k0, TK), :]
        out = []
        for hh in (0, 1):
            fk = f_ref[0, pl.ds(2 * pair + hh, 1), pl.ds(k0, TK)]
            s = _dot_nt(qhs[hh][r0:], k) - fk
            if mask is not None:
                s = jnp.where(mask, s, NEG_INF)
            s_ref[hh, r0:, pl.ds(k0, TK)] = s
            part = jnp.maximum(mx[hh][r0:], jnp.maximum(s[:, :LANES], s[:, LANES:]))
            out.append(_set_rows(mx[hh], r0, part))
        return tuple(out)

    mx = (jnp.full((TQ, LANES), NEG_INF, F32),) * 2
    mx = lax.fori_loop(0, n_full, lambda j, c: scores(j, c, 0, None), mx)
    for d in range(n_diag):
        mx = scores(n_full + d, mx, d * TK, _tri_mask(TQ - d * TK, False))
    ms = [jnp.max(x, axis=-1, keepdims=True) for x in mx]

    def weigh(j, acc, r0):
        k0 = pl.multiple_of(j * TK, TK)
        out = []
        for hh in (0, 1):
            pm = jnp.exp2(s_ref[hh, r0:, pl.ds(k0, TK)] - ms[hh][r0:])
            part = acc[hh][r0:] + _dot(pm.astype(BF), vxs[hh][pl.ds(k0, TK), :])
            out.append(_set_rows(acc[hh], r0, part))
        return tuple(out)

    acc = (jnp.zeros((TQ, LANES), F32),) * 2
    acc = lax.fori_loop(0, n_full, lambda j, c: weigh(j, c, 0), acc)
    for d in range(n_diag):
        acc = weigh(n_full + d, acc, d * TK)
    lane = lax.broadcasted_iota(jnp.int32, (TQ, LANES), 1)
    outs = [a / pltpu.roll(a, HEAD_DIM, 1) for a in acc]
    o_ref[0] = jnp.where(lane < HEAD_DIM, outs[0], outs[1]).astype(BF)


def _fox_call(qkv3, f_row):
    n_pairs = N_HEADS // 2
    return pl.pallas_call(
        _fox_kernel, grid=(BATCH, n_pairs, SEQ // TQ),
        in_specs=[pl.BlockSpec((1, TQ, LANES), lambda b, p, i: (b, i, p)),
                  pl.BlockSpec((1, SEQ, LANES), lambda b, p, i: (b, 0, n_pairs + p)),
                  pl.BlockSpec((1, SEQ, LANES), lambda b, p, i: (b, 0, 2 * n_pairs + p)),
                  pl.BlockSpec((1, N_HEADS, SEQ), lambda b, p, i: (b, 0, 0))],
        out_specs=pl.BlockSpec((1, TQ, LANES), lambda b, p, i: (b, i, p)),
        out_shape=jax.ShapeDtypeStruct((BATCH, SEQ, D_ATTN), BF),
        scratch_shapes=[pltpu.VMEM((SEQ, LANES), BF), pltpu.VMEM((SEQ, LANES), BF),
                        pltpu.VMEM((2, TQ, SEQ), F32)],
        compiler_params=pltpu.CompilerParams(
            dimension_semantics=("parallel", "arbitrary", "arbitrary"),
            vmem_limit_bytes=VMEM_LIMIT),
        name="fox_attn")(qkv3, qkv3, qkv3, f_row)


def _sb_kernel(q_ref, k_ref, v_ref, o_ref, u_ref):
    qi = pl.program_id(2)
    n_diag = TQ // TK
    n_full = qi * n_diag

    @pl.when(qi == 0)
    def _():
        r = lax.broadcasted_iota(jnp.int32, (2 * TK, TK), 0) % TK
        c = lax.broadcasted_iota(jnp.int32, (2 * TK, TK), 1)
        u_ref[...] = jnp.where(r > c, 1.0, 0.0).astype(BF)

    qhs = _head_split(q_ref[0])

    def blk(j, state, r0, mask):
        k0 = pl.multiple_of(j * TK, TK)
        k = k_ref[0, pl.ds(k0, TK), :]
        v = v_ref[0, pl.ds(k0, TK), :]
        u = u_ref[...]
        out = []
        for hh in (0, 1):
            carry, acc = state[hh]
            z = _dot_nt(qhs[hh][r0:], k)
            nz = -z
            lg = jnp.minimum(nz, 0.0) - jnp.log2(1.0 + jnp.exp2(jnp.minimum(z, nz)))
            arg = z + lg
            if mask is not None:
                lg = jnp.where(mask, lg, 0.0)
            lh = lg.astype(BF)
            ll = (lg - lh.astype(F32)).astype(BF)
            suf = _dot(jnp.concatenate([lh, ll], axis=1), u)
            a = jnp.exp2(arg + (suf + carry[r0:]))
            if mask is not None:
                a = jnp.where(mask, a, 0.0)
            acc = _set_rows(acc, r0, acc[r0:] + _dot(a.astype(BF), v))
            carry = _set_rows(carry, r0, carry[r0:] + jnp.sum(lg, axis=-1, keepdims=True))
            out.append((carry, acc))
        return tuple(out)

    state = ((jnp.zeros((TQ, 1), F32), jnp.zeros((TQ, LANES), F32)),) * 2
    for d in reversed(range(n_diag)):
        state = blk(n_full + d, state, d * TK, _tri_mask(TQ - d * TK, True))
    state = lax.fori_loop(0, n_full, lambda t, st: blk(n_full - 1 - t, st, 0, None), state)
    lane = lax.broadcasted_iota(jnp.int32, (TQ, LANES), 1)
    o_ref[0] = jnp.where(lane < HEAD_DIM, state[0][1], state[1][1]).astype(BF)


def _sb_call(qkv3):
    n_pairs = N_HEADS // 2
    n_sb = n_pairs // 2
    return pl.pallas_call(
        _sb_kernel, grid=(BATCH, n_sb, SEQ // TQ),
        in_specs=[pl.BlockSpec((1, TQ, LANES), lambda b, p, i: (b, i, p)),
                  pl.BlockSpec((1, SEQ, LANES), lambda b, p, i: (b, 0, n_pairs + p)),
                  pl.BlockSpec((1, SEQ, LANES), lambda b, p, i: (b, 0, 2 * n_pairs + p))],
        out_specs=pl.BlockSpec((1, TQ, LANES), lambda b, p, i: (b, i, p)),
        out_shape=jax.ShapeDtypeStruct((BATCH, SEQ, D_ATTN // 2), BF),
        scratch_shapes=[pltpu.VMEM((2 * TK, TK), BF)],
        compiler_params=pltpu.CompilerParams(
            dimension_semantics=("parallel", "arbitrary", "arbitrary"),
            vmem_limit_bytes=VMEM_LIMIT),
        name="stickbreak_attn")(qkv3, qkv3, qkv3)


def _dil_kernel(q_ref, k_ref, v_ref, c_ref, s1_ref, s2_ref, o_ref,
                nat_ref, qp_ref, kp_ref, vpa_ref, vpb_ref, x_ref, on_ref):
    lane = lax.broadcasted_iota(jnp.int32, (DB, LANES), 1)
    is_a = lane < HEAD_DIM

    half = ROT_DIM // 2

    def rot(x, rows):
        return (x * c_ref[rows, :] + pltpu.roll(x, LANES - half, 1) * s1_ref[rows, :]
                + pltpu.roll(x, half, 1) * s2_ref[rows, :])

    for t in range(N_DB):
        rows = slice(t * DB, (t + 1) * DB)
        nat_ref[0, rows, :] = rot(q_ref[0, rows, :].astype(F32), rows)
        nat_ref[1, rows, :] = rot(k_ref[0, rows, :].astype(F32), rows)
        nat_ref[2, rows, :] = v_ref[0, rows, :].astype(F32)

    for pi, (window, d) in enumerate(DIL_PATTERNS):
        per_res = SEQ // d // DB
        for t in range(N_DB):
            r, c = divmod(t, per_res)
            src = pl.ds(r + d * DB * c, DB, stride=d) if d > 1 else pl.ds(t * DB, DB)
            dst = slice(t * DB, (t + 1) * DB)
            qp_ref[pi, dst, :] = nat_ref[0, src, :].astype(BF)
            kp_ref[pi, dst, :] = nat_ref[1, src, :].astype(BF)
            v = nat_ref[2, src, :]
            vpa_ref[pi, dst, :] = jnp.where(is_a, v, 1.0).astype(BF)
            vpb_ref[pi, dst, :] = jnp.where(is_a, 1.0, v).astype(BF)

    row = lax.broadcasted_iota(jnp.int32, (DB, DB), 0)
    col = lax.broadcasted_iota(jnp.int32, (DB, DB), 1)
    cur_ok = col <= row

    def body(t, carry):
        r0 = pl.multiple_of(t * DB, DB)
        p0 = pl.multiple_of(jnp.maximum(t - 1, 0) * DB, DB)
        for pi, (window, d) in enumerate(DIL_PATTERNS):
            per_res = SEQ // d // DB
            qb = qp_ref[pi, pl.ds(r0, DB), :]
            kc = kp_ref[pi, pl.ds(r0, DB), :]
            if per_res > 1:
                kp = kp_ref[pi, pl.ds(p0, DB), :]
                prev_ok = col >= row + jnp.where((t % per_res) != 0, 0, DB)
            for hh, vx_ref in ((0, vpa_ref), (1, vpb_ref)):
                mine = is_a if hh == 0 else jnp.logical_not(is_a)
                q = jnp.where(mine, qb, jnp.zeros_like(qb))
                vc = vx_ref[pi, pl.ds(r0, DB), :]
                s_cur = jnp.where(cur_ok, _dot_nt(q, kc), NEG_INF)
                if per_res > 1:
                    vp = vx_ref[pi, pl.ds(p0, DB), :]
                    s_prev = jnp.where(prev_ok, _dot_nt(q, kp), NEG_INF)
                    m = jnp.max(jnp.maximum(s_cur, s_prev), axis=-1, keepdims=True)
                    acc = (_dot(jnp.exp2(s_cur - m).astype(BF), vc)
                           + _dot(jnp.exp2(s_prev - m).astype(BF), vp))
                else:
                    m = jnp.max(s_cur, axis=-1, keepdims=True)
                    acc = _dot(jnp.exp2(s_cur - m).astype(BF), vc)
                den = pltpu.roll(acc, HEAD_DIM, 1)
                lse = m + jnp.log2(jnp.maximum(acc, 1e-30))
                x_ref[pi, hh, pl.ds(r0, DB), :] = jnp.where(mine, acc / den, lse)
        return carry

    lax.fori_loop(0, N_DB, body, 0, unroll=DIL_UNROLL)

    d_max = DIL_PATTERNS[-1][1]
    for r in range(d_max):
        res = []
        for hh in (0, 1):
            xs = []
            for pi, (window, d) in enumerate(DIL_PATTERNS):
                step = d_max // d
                start = (r % d) * (SEQ // d) + r // d
                rows = pl.ds(start, DB, stride=step) if step > 1 else pl.ds(start, DB)
                xs.append(x_ref[pi, hh, rows, :])
            ls = [pltpu.roll(x, HEAD_DIM, 1) for x in xs]
            mx = functools.reduce(jnp.maximum, ls)
            es = [jnp.exp2(l - mx) for l in ls]
            inv = 1.0 / functools.reduce(lambda a, b: a + b, es)
            res.append(functools.reduce(lambda a, b: a + b, [(e * inv) * x for e, x in zip(es, xs)]))
        on_ref[pl.ds(r, DB, stride=d_max), :] = jnp.where(is_a, res[0], res[1])

    o_ref[0] = on_ref[...].astype(BF)


def _dil_call(qkv3, cos_t, sin1_t, sin2_t):
    n_pairs = N_HEADS // 2
    n_sb = n_pairs // 2
    n_pat = len(DIL_PATTERNS)
    tab = pl.BlockSpec((SEQ, LANES), lambda b, p: (0, 0))
    return pl.pallas_call(
        _dil_kernel, grid=(BATCH, n_pairs - n_sb),
        in_specs=[pl.BlockSpec((1, SEQ, LANES), lambda b, p: (b, 0, n_sb + p)),
                  pl.BlockSpec((1, SEQ, LANES), lambda b, p: (b, 0, n_pairs + n_sb + p)),
                  pl.BlockSpec((1, SEQ, LANES), lambda b, p: (b, 0, 2 * n_pairs + n_sb + p)),
                  tab, tab, tab],
        out_specs=pl.BlockSpec((1, SEQ, LANES), lambda b, p: (b, 0, p)),
        out_shape=jax.ShapeDtypeStruct((BATCH, SEQ, D_ATTN // 2), BF),
        scratch_shapes=[pltpu.VMEM((3, SEQ, LANES), F32),
                        pltpu.VMEM((n_pat, SEQ, LANES), BF),
                        pltpu.VMEM((n_pat, SEQ, LANES), BF),
                        pltpu.VMEM((n_pat, SEQ, LANES), BF),
                        pltpu.VMEM((n_pat, SEQ, LANES), BF),
                        pltpu.VMEM((n_pat, 2, SEQ, LANES), F32),
                        pltpu.VMEM((SEQ, LANES), F32)],
        compiler_params=pltpu.CompilerParams(dimension_semantics=("parallel", "arbitrary"),
                                             vmem_limit_bytes=VMEM_LIMIT),
        name="dilated_attn")(qkv3, qkv3, qkv3, cos_t, sin1_t, sin2_t)


def _rotary_tables():
    half = ROT_DIM // 2
    pos = jnp.arange(SEQ, dtype=F32)
    inv_freq = ROPE_THETA ** (-jnp.arange(half, dtype=F32) * 2.0 / ROT_DIM)
    ang = pos[:, None] * inv_freq[None, :]
    cos, sin = jnp.cos(ang), jnp.sin(ang)
    zeros = jnp.zeros((SEQ, HEAD_DIM - ROT_DIM), F32)
    z8 = jnp.zeros((SEQ, half), F32)
    c_head = jnp.concatenate([cos, cos, zeros + 1.0], axis=1)
    s1_head = jnp.concatenate([-sin, z8, zeros], axis=1)
    s2_head = jnp.concatenate([z8, sin, zeros], axis=1)
    two = lambda t: jnp.concatenate([t, t], axis=1)
    return two(c_head), two(s1_head), two(s2_head)


def kernel(x, norm_mix, w_qkv_even, w_o_even, w_qkvf_odd, b_forget, w_o_odd, norm_ffn,
           w_ffn_in, w_ffn_out, norm_final):
    h = x.reshape(M_TOKENS, D_MODEL)
    cos_t, sin1_t, sin2_t = _rotary_tables()
    n_qkv = 3 * D_ATTN
    for layer in range(DEPTH):
        i = layer // 2
        if layer % 2 == 0:
            qkv = _qkv_call(h, norm_mix[layer], w_qkv_even[i].astype(BF))
            qkv3 = qkv.reshape(BATCH, SEQ, n_qkv)
            oa = _sb_call(qkv3).reshape(M_TOKENS, D_ATTN // 2)
            ob = _dil_call(qkv3, cos_t, sin1_t, sin2_t).reshape(M_TOKENS, D_ATTN // 2)
            ob_col = 0
            wo = w_o_even[i]
        else:
            w = w_qkvf_odd[i]
            wf = jnp.pad(w[:, n_qkv:], ((0, 0), (0, LANES - N_HEADS))).astype(BF)
            qkv, flog = _qkv_call(h, norm_mix[layer], w[:, :n_qkv].astype(BF), wf)
            bias = jnp.pad(b_forget[i], (0, LANES - N_HEADS)).reshape(1, LANES)
            f_row = _fprep_call(flog.reshape(BATCH, SEQ, LANES), bias)
            o = _fox_call(qkv.reshape(BATCH, SEQ, n_qkv), f_row).reshape(M_TOKENS, D_ATTN)
            oa, ob, ob_col = o, o, 1
            wo = w_o_odd[i]
        h = _ffn_call(h, oa, ob, ob_col, wo.astype(BF), norm_ffn[layer],
                      w_ffn_in[layer].astype(BF), w_ffn_out[layer].astype(BF), norm_final,
                      final_norm=(layer == DEPTH - 1))
    return h.reshape(BATCH, SEQ, D_MODEL)
```

```python
import functools
import math

import jax
import jax.numpy as jnp
from jax import lax
from jax.experimental import pallas as pl
from jax.experimental.pallas import tpu as pltpu

D_MODEL = 1024
BATCH = 8
SEQ = 2048
DEPTH = 4
HEAD_DIM = 64
N_HEADS = 16
D_ATTN = N_HEADS * HEAD_DIM
D_FF = 2816
ROPE_THETA = 500000.0
ROT_DIM = HEAD_DIM // 4
DIL_PATTERNS = ((128, 1), (512, 4), (2048, 16))
RMS_EPS = 1e-5
LOG2E = math.log2(math.e)
Q_SCALE = HEAD_DIM ** -0.5 * LOG2E

LANES = 128
M_TOKENS = BATCH * SEQ
TM = 512
TN = 1024
TF = 256
TQ = 1024
TK = 256
RC = 256
DB = 128
N_DB = SEQ // DB
DIL_UNROLL = 2
VMEM_LIMIT = 56 * 1024 * 1024

BF = jnp.bfloat16
F32 = jnp.float32
NEG_INF = float("-inf")


def _rms(x, g):
    ms = jnp.mean(x * x, axis=-1, keepdims=True)
    return x * lax.rsqrt(ms + RMS_EPS) * g


def _dot(a, b):
    return jnp.dot(a, b, preferred_element_type=F32)


def _dot_nt(a, b):
    return lax.dot_general(a, b, (((1,), (1,)), ((), ())), preferred_element_type=F32)


def _set_rows(full, r0, part):
    return part if r0 == 0 else jnp.concatenate([full[:r0], part], axis=0)


def _qkv_kernel(x_ref, g_ref, w_ref, *rest):
    hn = _rms(x_ref[...], g_ref[...]).astype(BF)
    o_ref = rest[-1] if len(rest) == 1 else rest[1]
    for j in range(3 * D_ATTN // TN):
        y = _dot(hn, w_ref[:, j * TN:(j + 1) * TN])
        if (j + 1) * TN <= D_ATTN:
            y = y * Q_SCALE
        o_ref[:, j * TN:(j + 1) * TN] = y.astype(BF)
    if len(rest) == 3:
        wf_ref, _, f_ref = rest
        f_ref[...] = _dot(hn, wf_ref[...])


def _const_spec(shape):
    return pl.BlockSpec(shape, lambda *_: (0,) * len(shape), pipeline_mode=pl.Buffered(1))


def _qkv_call(x, g, w, wf=None):
    n = 3 * D_ATTN
    in_specs = [pl.BlockSpec((TM, D_MODEL), lambda i: (i, 0)),
                _const_spec((1, D_MODEL)),
                _const_spec((D_MODEL, n))]
    out_specs = pl.BlockSpec((TM, n), lambda i: (i, 0))
    out_shape = jax.ShapeDtypeStruct((M_TOKENS, n), BF)
    args = [x, g.reshape(1, D_MODEL), w]
    if wf is not None:
        in_specs.append(_const_spec((D_MODEL, LANES)))
        out_specs = [out_specs, pl.BlockSpec((TM, LANES), lambda i: (i, 0))]
        out_shape = [out_shape, jax.ShapeDtypeStruct((M_TOKENS, LANES), F32)]
        args.append(wf)
    return pl.pallas_call(
        _qkv_kernel, grid=(M_TOKENS // TM,), in_specs=in_specs, out_specs=out_specs,
        out_shape=out_shape,
        compiler_params=pltpu.CompilerParams(dimension_semantics=("parallel",),
                                             vmem_limit_bytes=VMEM_LIMIT),
        name="norm_qkv" if wf is None else "norm_qkvf")(*args)


def _ffn_kernel(x_ref, oa_ref, ob_ref, wo_ref, g_ref, win_ref, wout_ref, gf_ref, out_ref,
                acc_ref, hn_ref, *, final_norm):
    half = D_ATTN // 2
    xn = x_ref[...] + _dot(oa_ref[...], wo_ref[0:half, :]) + _dot(ob_ref[...], wo_ref[half:, :])
    acc_ref[...] = xn
    hn_ref[...] = _rms(xn, g_ref[...]).astype(BF)

    def body(f, c):
        c0 = pl.multiple_of(f * TF, TF)
        c1 = pl.multiple_of(D_FF + f * TF, TF)
        hn = hn_ref[...]
        g = _dot(hn, win_ref[:, pl.ds(c0, TF)])
        u = _dot(hn, win_ref[:, pl.ds(c1, TF)])
        a = (g * (1.0 / (1.0 + jnp.exp(-g))) * u).astype(BF)
        acc_ref[...] += _dot(a, wout_ref[pl.ds(c0, TF), :])
        return c

    lax.fori_loop(0, D_FF // TF, body, 0)
    y = acc_ref[...]
    if final_norm:
        y = _rms(y, gf_ref[...])
    out_ref[...] = y


def _ffn_call(x, oa, ob, ob_col, wo, g, win, wout, gf, final_norm):
    half = D_ATTN // 2
    in_specs = [pl.BlockSpec((TM, D_MODEL), lambda i: (i, 0)),
                pl.BlockSpec((TM, half), lambda i: (i, 0)),
                pl.BlockSpec((TM, half), lambda i: (i, ob_col)),
                _const_spec((D_ATTN, D_MODEL)),
                _const_spec((1, D_MODEL)),
                _const_spec((D_MODEL, 2 * D_FF)),
                _const_spec((D_FF, D_MODEL)),
                _const_spec((1, D_MODEL))]
    return pl.pallas_call(
        functools.partial(_ffn_kernel, final_norm=final_norm),
        grid=(M_TOKENS // TM,), in_specs=in_specs,
        out_specs=pl.BlockSpec((TM, D_MODEL), lambda i: (i, 0)),
        out_shape=jax.ShapeDtypeStruct((M_TOKENS, D_MODEL), F32),
        scratch_shapes=[pltpu.VMEM((TM, D_MODEL), F32), pltpu.VMEM((TM, D_MODEL), BF)],
        compiler_params=pltpu.CompilerParams(dimension_semantics=("parallel",),
                                             vmem_limit_bytes=VMEM_LIMIT),
        name="oproj_ffn")(x, oa, ob, wo, g.reshape(1, D_MODEL), win, wout, gf.reshape(1, D_MODEL))


def _fprep_kernel(fl_ref, b_ref, o_ref):
    x = fl_ref[0] + b_ref[...]
    lf = jnp.minimum(x, 0.0) - jnp.log1p(jnp.exp(-jnp.abs(x)))
    lft = lf.T
    r = lax.broadcasted_iota(jnp.int32, (TK, TK), 0)
    c = lax.broadcasted_iota(jnp.int32, (TK, TK), 1)
    u = jnp.where(r <= c, 1.0, 0.0).astype(BF)
    carry = jnp.zeros((N_HEADS, 1), F32)
    for cb in range(SEQ // TK):
        blk = lft[0:N_HEADS, cb * TK:(cb + 1) * TK]
        b0 = blk.astype(BF)
        r1 = blk - b0.astype(F32)
        b1 = r1.astype(BF)
        b2 = (r1 - b1.astype(F32)).astype(BF)
        cs = _dot(b0, u) + _dot(b1, u) + _dot(b2, u) + carry
        o_ref[0, :, cb * TK:(cb + 1) * TK] = cs * LOG2E
        carry = cs[:, TK - 1:TK]


def _fprep_call(flog, bias):
    return pl.pallas_call(
        _fprep_kernel, grid=(BATCH,),
        in_specs=[pl.BlockSpec((1, SEQ, LANES), lambda b: (b, 0, 0)),
                  pl.BlockSpec((1, LANES), lambda b: (0, 0))],
        out_specs=pl.BlockSpec((1, N_HEADS, SEQ), lambda b: (b, 0, 0)),
        out_shape=jax.ShapeDtypeStruct((BATCH, N_HEADS, SEQ), F32),
        compiler_params=pltpu.CompilerParams(dimension_semantics=("parallel",),
                                             vmem_limit_bytes=VMEM_LIMIT),
        name="forget_cumsum")(flog, bias)


def _tri_mask(rows, strict, row0=0):
    row = lax.broadcasted_iota(jnp.int32, (rows, TK), 0) + row0
    col = lax.broadcasted_iota(jnp.int32, (rows, TK), 1)
    return (col < row) if strict else (col <= row)


def _head_split(q):
    lane = lax.broadcasted_iota(jnp.int32, q.shape, 1)
    zero = jnp.zeros_like(q)
    return jnp.where(lane < HEAD_DIM, q, zero), jnp.where(lane >= HEAD_DIM, q, zero)


def _fox_kernel(q_ref, k_ref, v_ref, f_ref, o_ref, va_ref, vb_ref):
    pair = pl.program_id(1)
    lane_kv = lax.broadcasted_iota(jnp.int32, (SEQ, LANES), 1)
    v = v_ref[0]
    one = jnp.ones_like(v)
    va_ref[...] = jnp.where(lane_kv < HEAD_DIM, v, one)
    vb_ref[...] = jnp.where(lane_kv >= HEAD_DIM, v, one)
    vxs = (va_ref, vb_ref)
    diag_ok = _tri_mask(RC, False)
    lane = lax.broadcasted_iota(jnp.int32, (RC, LANES), 1)

    def qk(c):
        qh = _head_split(q_ref[0, c * RC:(c + 1) * RC, :])
        k = k_ref[0, 0:(c + 1) * RC, :]
        return [_dot_nt(qh[hh], k) for hh in (0, 1)]

    def softmax(c, zs):
        kend = (c + 1) * RC
        ps = []
        for hh in (0, 1):
            s = zs[hh] - f_ref[0, pl.ds(2 * pair + hh, 1), 0:kend]
            tail = jnp.where(diag_ok, s[:, kend - RC:], NEG_INF)
            s = tail if c == 0 else jnp.concatenate([s[:, :kend - RC], tail], axis=1)
            ps.append(jnp.exp2(s - jnp.max(s, axis=-1, keepdims=True)).astype(BF))
        return ps

    def pv(c, ps):
        outs = []
        for hh in (0, 1):
            o = _dot(ps[hh], vxs[hh][0:(c + 1) * RC, :])
            outs.append(o / pltpu.roll(o, HEAD_DIM, 1))
        o_ref[0, c * RC:(c + 1) * RC, :] = jnp.where(lane < HEAD_DIM, outs[0], outs[1]).astype(BF)

    n_chunks = SEQ // RC
    zs = qk(0)
    for c in range(n_chunks):
        zs_next = qk(c + 1) if c + 1 < n_chunks else None
        pv(c, softmax(c, zs))
        zs = zs_next


def _fox_call(qkv3, f_row):
    n_pairs = N_HEADS // 2
    return pl.pallas_call(
        _fox_kernel, grid=(BATCH, n_pairs),
        in_specs=[pl.BlockSpec((1, SEQ, LANES), lambda b, p: (b, 0, p)),
                  pl.BlockSpec((1, SEQ, LANES), lambda b, p: (b, 0, n_pairs + p)),
                  pl.BlockSpec((1, SEQ, LANES), lambda b, p: (b, 0, 2 * n_pairs + p)),
                  pl.BlockSpec((1, N_HEADS, SEQ), lambda b, p: (b, 0, 0))],
        out_specs=pl.BlockSpec((1, SEQ, LANES), lambda b, p: (b, 0, p)),
        out_shape=jax.ShapeDtypeStruct((BATCH, SEQ, D_ATTN), BF),
        scratch_shapes=[pltpu.VMEM((SEQ, LANES), BF), pltpu.VMEM((SEQ, LANES), BF)],
        compiler_params=pltpu.CompilerParams(
            dimension_semantics=("parallel", "arbitrary"),
            vmem_limit_bytes=VMEM_LIMIT),
        name="fox_attn")(qkv3, qkv3, qkv3, f_row)


def _sb_kernel(q_ref, k_ref, v_ref, o_ref, u_ref):
    r = lax.broadcasted_iota(jnp.int32, (TK, TK), 0)
    c = lax.broadcasted_iota(jnp.int32, (TK, TK), 1)
    u_ref[...] = jnp.where(r > c, 1.0, 0.0)
    strict = _tri_mask(RC, True)
    lane = lax.broadcasted_iota(jnp.int32, (RC, LANES), 1)
    heads = (0, 1)

    def qk(c):
        qh = _head_split(q_ref[0, c * RC:(c + 1) * RC, :])
        k = k_ref[0, 0:(c + 1) * RC, :]
        return [_dot_nt(qh[hh], k) for hh in heads]

    def gates(c, zs):
        nb = c + 1
        out = []
        for z in zs:
            nz = -z
            lg = jnp.minimum(nz, 0.0) - jnp.log2(1.0 + jnp.exp2(jnp.minimum(z, nz)))
            arg = z + lg
            blocks = [lg[:, j * TK:(j + 1) * TK] for j in range(nb)]
            blocks[-1] = jnp.where(strict, blocks[-1], 0.0)
            carries, run = [None] * nb, None
            for j in reversed(range(nb)):
                carries[j] = run
                rs = jnp.sum(blocks[j], axis=-1, keepdims=True)
                run = rs if run is None else run + rs
            stacked = blocks[0] if nb == 1 else jnp.concatenate(blocks, axis=0)
            out.append((arg, _dot(stacked, u_ref[...]), carries))
        return out

    def values(c, gs):
        nb = c + 1
        outs = []
        for arg, suf, carries in gs:
            ws = []
            for j in range(nb):
                e = suf[j * RC:(j + 1) * RC]
                if carries[j] is not None:
                    e = e + carries[j]
                ws.append(jnp.exp2(arg[:, j * TK:(j + 1) * TK] + e))
            ws[-1] = jnp.where(strict, ws[-1], 0.0)
            a = (ws[0] if nb == 1 else jnp.concatenate(ws, axis=1)).astype(BF)
            outs.append(_dot(a, v_ref[0, 0:nb * TK, :]))
        o_ref[0, c * RC:(c + 1) * RC, :] = jnp.where(lane < HEAD_DIM, outs[0], outs[1]).astype(BF)

    n_chunks = SEQ // RC
    zs = {0: qk(0)}
    gs = {0: gates(0, zs.pop(0))}
    if n_chunks > 1:
        zs[1] = qk(1)
    for c in range(n_chunks):
        if c + 2 < n_chunks:
            zs[c + 2] = qk(c + 2)
        if c + 1 < n_chunks:
            gs[c + 1] = gates(c + 1, zs.pop(c + 1))
        values(c, gs.pop(c))


def _sb_call(qkv3):
    n_pairs = N_HEADS // 2
    n_sb = n_pairs // 2
    return pl.pallas_call(
        _sb_kernel, grid=(BATCH, n_sb),
        in_specs=[pl.BlockSpec((1, SEQ, LANES), lambda b, p: (b, 0, p)),
                  pl.BlockSpec((1, SEQ, LANES), lambda b, p: (b, 0, n_pairs + p)),
                  pl.BlockSpec((1, SEQ, LANES), lambda b, p: (b, 0, 2 * n_pairs + p))],
        out_specs=pl.BlockSpec((1, SEQ, LANES), lambda b, p: (b, 0, p)),
        out_shape=jax.ShapeDtypeStruct((BATCH, SEQ, D_ATTN // 2), BF),
        scratch_shapes=[pltpu.VMEM((TK, TK), F32)],
        compiler_params=pltpu.CompilerParams(
            dimension_semantics=("parallel", "arbitrary"),
            vmem_limit_bytes=VMEM_LIMIT),
        name="stickbreak_attn")(qkv3, qkv3, qkv3)


def _dil_kernel(q_ref, k_ref, v_ref, c_ref, s1_ref, s2_ref, o_ref,
                nat_ref, qp_ref, kp_ref, vpa_ref, vpb_ref, x_ref, on_ref):
    lane = lax.broadcasted_iota(jnp.int32, (DB, LANES), 1)
    is_a = lane < HEAD_DIM

    half = ROT_DIM // 2

    def rot(x, rows):
        return (x * c_ref[rows, :] + pltpu.roll(x, LANES - half, 1) * s1_ref[rows, :]
                + pltpu.roll(x, half, 1) * s2_ref[rows, :])

    for t in range(N_DB):
        rows = slice(t * DB, (t + 1) * DB)
        nat_ref[0, rows, :] = rot(q_ref[0, rows, :].astype(F32), rows)
        nat_ref[1, rows, :] = rot(k_ref[0, rows, :].astype(F32), rows)
        nat_ref[2, rows, :] = v_ref[0, rows, :].astype(F32)

    for pi, (window, d) in enumerate(DIL_PATTERNS):
        per_res = SEQ // d // DB
        for t in range(N_DB):
            r, c = divmod(t, per_res)
            src = pl.ds(r + d * DB * c, DB, stride=d) if d > 1 else pl.ds(t * DB, DB)
            dst = slice(t * DB, (t + 1) * DB)
            qp_ref[pi, dst, :] = nat_ref[0, src, :].astype(BF)
            kp_ref[pi, dst, :] = nat_ref[1, src, :].astype(BF)
            v = nat_ref[2, src, :]
            vpa_ref[pi, dst, :] = jnp.where(is_a, v, 1.0).astype(BF)
            vpb_ref[pi, dst, :] = jnp.where(is_a, 1.0, v).astype(BF)

    row = lax.broadcasted_iota(jnp.int32, (DB, DB), 0)
    col = lax.broadcasted_iota(jnp.int32, (DB, DB), 1)
    cur_ok = col <= row

    chains = [(pi, hh) for pi in range(len(DIL_PATTERNS)) for hh in (0, 1)]

    def body(t, carry):
        r0 = pl.multiple_of(t * DB, DB)
        p0 = pl.multiple_of(jnp.maximum(t - 1, 0) * DB, DB)
        scores = []
        for pi, hh in chains:
            per_res = SEQ // DIL_PATTERNS[pi][1] // DB
            mine = is_a if hh == 0 else jnp.logical_not(is_a)
            qb = qp_ref[pi, pl.ds(r0, DB), :]
            q = jnp.where(mine, qb, jnp.zeros_like(qb))
            s = [_dot_nt(q, kp_ref[pi, pl.ds(r0, DB), :])]
            if per_res > 1:
                s.append(_dot_nt(q, kp_ref[pi, pl.ds(p0, DB), :]))
            scores.append(s)
        probs, maxes = [], []
        for (pi, hh), s in zip(chains, scores):
            per_res = SEQ // DIL_PATTERNS[pi][1] // DB
            s_cur = jnp.where(cur_ok, s[0], NEG_INF)
            if per_res > 1:
                prev_ok = col >= row + jnp.where((t % per_res) != 0, 0, DB)
                s_prev = jnp.where(prev_ok, s[1], NEG_INF)
                m = jnp.max(jnp.maximum(s_cur, s_prev), axis=-1, keepdims=True)
                p = [jnp.exp2(s_cur - m).astype(BF), jnp.exp2(s_prev - m).astype(BF)]
            else:
                m = jnp.max(s_cur, axis=-1, keepdims=True)
                p = [jnp.exp2(s_cur - m).astype(BF)]
            probs.append(p)
            maxes.append(m)
        accs = []
        for (pi, hh), p in zip(chains, probs):
            vx_ref = vpa_ref if hh == 0 else vpb_ref
            acc = _dot(p[0], vx_ref[pi, pl.ds(r0, DB), :])
            if len(p) > 1:
                acc = acc + _dot(p[1], vx_ref[pi, pl.ds(p0, DB), :])
            accs.append(acc)
        for (pi, hh), acc, m in zip(chains, accs, maxes):
            mine = is_a if hh == 0 else jnp.logical_not(is_a)
            den = pltpu.roll(acc, HEAD_DIM, 1)
            lse = m + jnp.log2(jnp.maximum(acc, 1e-30))
            x_ref[pi, hh, pl.ds(r0, DB), :] = jnp.where(mine, acc / den, lse)
        return carry

    lax.fori_loop(0, N_DB, body, 0, unroll=DIL_UNROLL)

    d_max = DIL_PATTERNS[-1][1]
    for r in range(d_max):
        res = []
        for hh in (0, 1):
            xs = []
            for pi, (window, d) in enumerate(DIL_PATTERNS):
                step = d_max // d
                start = (r % d) * (SEQ // d) + r // d
                rows = pl.ds(start, DB, stride=step) if step > 1 else pl.ds(start, DB)
                xs.append(x_ref[pi, hh, rows, :])
            ls = [pltpu.roll(x, HEAD_DIM, 1) for x in xs]
            mx = functools.reduce(jnp.maximum, ls)
            es = [jnp.exp2(l - mx) for l in ls]
            inv = 1.0 / functools.reduce(lambda a, b: a + b, es)
            res.append(functools.reduce(lambda a, b: a + b, [(e * inv) * x for e, x in zip(es, xs)]))
        on_ref[pl.ds(r, DB, stride=d_max), :] = jnp.where(is_a, res[0], res[1])

    o_ref[0] = on_ref[...].astype(BF)


def _dil_call(qkv3, cos_t, sin1_t, sin2_t):
    n_pairs = N_HEADS // 2
    n_sb = n_pairs // 2
    n_pat = len(DIL_PATTERNS)
    tab = pl.BlockSpec((SEQ, LANES), lambda b, p: (0, 0))
    return pl.pallas_call(
        _dil_kernel, grid=(BATCH, n_pairs - n_sb),
        in_specs=[pl.BlockSpec((1, SEQ, LANES), lambda b, p: (b, 0, n_sb + p)),
                  pl.BlockSpec((1, SEQ, LANES), lambda b, p: (b, 0, n_pairs + n_sb + p)),
                  pl.BlockSpec((1, SEQ, LANES), lambda b, p: (b, 0, 2 * n_pairs + n_sb + p)),
                  tab, tab, tab],
        out_specs=pl.BlockSpec((1, SEQ, LANES), lambda b, p: (b, 0, p)),
        out_shape=jax.ShapeDtypeStruct((BATCH, SEQ, D_ATTN // 2), BF),
        scratch_shapes=[pltpu.VMEM((3, SEQ, LANES), F32),
                        pltpu.VMEM((n_pat, SEQ, LANES), BF),
                        pltpu.VMEM((n_pat, SEQ, LANES), BF),
                        pltpu.VMEM((n_pat, SEQ, LANES), BF),
                        pltpu.VMEM((n_pat, SEQ, LANES), BF),
                        pltpu.VMEM((n_pat, 2, SEQ, LANES), F32),
                        pltpu.VMEM((SEQ, LANES), F32)],
        compiler_params=pltpu.CompilerParams(dimension_semantics=("parallel", "arbitrary"),
                                             vmem_limit_bytes=VMEM_LIMIT),
        name="dilated_attn")(qkv3, qkv3, qkv3, cos_t, sin1_t, sin2_t)


def _rotary_tables():
    half = ROT_DIM // 2
    pos = jnp.arange(SEQ, dtype=F32)
    inv_freq = ROPE_THETA ** (-jnp.arange(half, dtype=F32) * 2.0 / ROT_DIM)
    ang = pos[:, None] * inv_freq[None, :]
    cos, sin = jnp.cos(ang), jnp.sin(ang)
    zeros = jnp.zeros((SEQ, HEAD_DIM - ROT_DIM), F32)
    z8 = jnp.zeros((SEQ, half), F32)
    c_head = jnp.concatenate([cos, cos, zeros + 1.0], axis=1)
    s1_head = jnp.concatenate([-sin, z8, zeros], axis=1)
    s2_head = jnp.concatenate([z8, sin, zeros], axis=1)
    two = lambda t: jnp.concatenate([t, t], axis=1)
    return two(c_head), two(s1_head), two(s2_head)


def kernel(x, norm_mix, w_qkv_even, w_o_even, w_qkvf_odd, b_forget, w_o_odd, norm_ffn,
           w_ffn_in, w_ffn_out, norm_final):
    h = x.reshape(M_TOKENS, D_MODEL)
    cos_t, sin1_t, sin2_t = _rotary_tables()
    n_qkv = 3 * D_ATTN
    for layer in range(DEPTH):
        i = layer // 2
        if layer % 2 == 0:
            qkv = _qkv_call(h, norm_mix[layer], w_qkv_even[i].astype(BF))
            qkv3 = qkv.reshape(BATCH, SEQ, n_qkv)
            oa = _sb_call(qkv3).reshape(M_TOKENS, D_ATTN // 2)
            ob = _dil_call(qkv3, cos_t, sin1_t, sin2_t).reshape(M_TOKENS, D_ATTN // 2)
            ob_col = 0
            wo = w_o_even[i]
        else:
            w = w_qkvf_odd[i]
            wf = jnp.pad(w[:, n_qkv:], ((0, 0), (0, LANES - N_HEADS))).astype(BF)
            qkv, flog = _qkv_call(h, norm_mix[layer], w[:, :n_qkv].astype(BF), wf)
            bias = jnp.pad(b_forget[i], (0, LANES - N_HEADS)).reshape(1, LANES)
            f_row = _fprep_call(flog.reshape(BATCH, SEQ, LANES), bias)
            o = _fox_call(qkv.reshape(BATCH, SEQ, n_qkv), f_row).reshape(M_TOKENS, D_ATTN)
            oa, ob, ob_col = o, o, 1
            wo = w_o_odd[i]
        h = _ffn_call(h, oa, ob, ob_col, wo.astype(BF), norm_ffn[layer],
                      w_ffn_in[layer].astype(BF), w_ffn_out[layer].astype(BF), norm_final,
                      final_norm=(layer == DEPTH - 1))
    return h.reshape(BATCH, SEQ, D_MODEL)
```

```python
import functools
import math

import jax
import jax.numpy as jnp
from jax import lax
from jax.experimental import pallas as pl
from jax.experimental.pallas import tpu as pltpu

D_MODEL = 1024
BATCH = 8
SEQ = 2048
DEPTH = 4
HEAD_DIM = 64
N_HEADS = 16
D_ATTN = N_HEADS * HEAD_DIM
D_FF = 2816
ROPE_THETA = 500000.0
ROT_DIM = HEAD_DIM // 4
DIL_PATTERNS = ((128, 1), (512, 4), (2048, 16))
RMS_EPS = 1e-5
LOG2E = math.log2(math.e)
Q_SCALE = HEAD_DIM ** -0.5 * LOG2E

LANES = 128
M_TOKENS = BATCH * SEQ
TM = 512
TM_FFN = 1024
TN = 1024
TF = 256
TQ = 1024
TK = 256
RC = 256
DB = 128
N_DB = SEQ // DB
DIL_GROUP = 8
VMEM_LIMIT = 56 * 1024 * 1024

BF = jnp.bfloat16
F32 = jnp.float32
NEG_INF = float("-inf")


def _rms(x, g):
    ms = jnp.mean(x * x, axis=-1, keepdims=True)
    return x * lax.rsqrt(ms + RMS_EPS) * g


def _dot(a, b):
    return jnp.dot(a, b, preferred_element_type=F32)


def _dot_nt(a, b):
    return lax.dot_general(a, b, (((1,), (1,)), ((), ())), preferred_element_type=F32)


def _set_rows(full, r0, part):
    return part if r0 == 0 else jnp.concatenate([full[:r0], part], axis=0)


def _qkv_kernel(x_ref, g_ref, w_ref, *rest):
    hn = _rms(x_ref[...], g_ref[...]).astype(BF)
    o_ref = rest[-1] if len(rest) == 1 else rest[1]
    for j in range(3 * D_ATTN // TN):
        y = _dot(hn, w_ref[:, j * TN:(j + 1) * TN])
        if (j + 1) * TN <= D_ATTN:
            y = y * Q_SCALE
        o_ref[:, j * TN:(j + 1) * TN] = y.astype(BF)
    if len(rest) == 3:
        wf_ref, _, f_ref = rest
        f_ref[...] = _dot(hn, wf_ref[...])


def _const_spec(shape):
    return pl.BlockSpec(shape, lambda *_: (0,) * len(shape), pipeline_mode=pl.Buffered(1))


def _qkv_call(x, g, w, wf=None):
    n = 3 * D_ATTN
    in_specs = [pl.BlockSpec((TM, D_MODEL), lambda i: (i, 0)),
                _const_spec((1, D_MODEL)),
                _const_spec((D_MODEL, n))]
    out_specs = pl.BlockSpec((TM, n), lambda i: (i, 0))
    out_shape = jax.ShapeDtypeStruct((M_TOKENS, n), BF)
    args = [x, g.reshape(1, D_MODEL), w]
    if wf is not None:
        in_specs.append(_const_spec((D_MODEL, LANES)))
        out_specs = [out_specs, pl.BlockSpec((TM, LANES), lambda i: (i, 0))]
        out_shape = [out_shape, jax.ShapeDtypeStruct((M_TOKENS, LANES), F32)]
        args.append(wf)
    return pl.pallas_call(
        _qkv_kernel, grid=(M_TOKENS // TM,), in_specs=in_specs, out_specs=out_specs,
        out_shape=out_shape,
        compiler_params=pltpu.CompilerParams(dimension_semantics=("parallel",),
                                             vmem_limit_bytes=VMEM_LIMIT),
        name="norm_qkv" if wf is None else "norm_qkvf")(*args)


def _ffn_kernel(x_ref, oa_ref, ob_ref, wo_ref, g_ref, win_ref, wout_ref, gf_ref, out_ref,
                acc_ref, hn_ref, *, final_norm):
    half = D_ATTN // 2
    xn = x_ref[...] + _dot(oa_ref[...], wo_ref[0:half, :]) + _dot(ob_ref[...], wo_ref[half:, :])
    acc_ref[...] = xn
    hn_ref[...] = _rms(xn, g_ref[...]).astype(BF)

    def body(f, c):
        c0 = pl.multiple_of(f * TF, TF)
        c1 = pl.multiple_of(D_FF + f * TF, TF)
        hn = hn_ref[...]
        g = _dot(hn, win_ref[:, pl.ds(c0, TF)])
        u = _dot(hn, win_ref[:, pl.ds(c1, TF)])
        a = (g * (1.0 / (1.0 + jnp.exp(-g))) * u).astype(BF)
        acc_ref[...] += _dot(a, wout_ref[pl.ds(c0, TF), :])
        return c

    lax.fori_loop(0, D_FF // TF, body, 0)
    y = acc_ref[...]
    if final_norm:
        y = _rms(y, gf_ref[...])
    out_ref[...] = y


def _ffn_call(x, oa, ob, ob_col, wo, g, win, wout, gf, final_norm):
    half = D_ATTN // 2
    in_specs = [pl.BlockSpec((TM_FFN, D_MODEL), lambda i: (i, 0)),
                pl.BlockSpec((TM_FFN, half), lambda i: (i, 0)),
                pl.BlockSpec((TM_FFN, half), lambda i: (i, ob_col)),
                _const_spec((D_ATTN, D_MODEL)),
                _const_spec((1, D_MODEL)),
                _const_spec((D_MODEL, 2 * D_FF)),
                _const_spec((D_FF, D_MODEL)),
                _const_spec((1, D_MODEL))]
    return pl.pallas_call(
        functools.partial(_ffn_kernel, final_norm=final_norm),
        grid=(M_TOKENS // TM_FFN,), in_specs=in_specs,
        out_specs=pl.BlockSpec((TM_FFN, D_MODEL), lambda i: (i, 0)),
        out_shape=jax.ShapeDtypeStruct((M_TOKENS, D_MODEL), F32),
        scratch_shapes=[pltpu.VMEM((TM_FFN, D_MODEL), F32), pltpu.VMEM((TM_FFN, D_MODEL), BF)],
        compiler_params=pltpu.CompilerParams(dimension_semantics=("parallel",),
                                             vmem_limit_bytes=VMEM_LIMIT),
        name="oproj_ffn")(x, oa, ob, wo, g.reshape(1, D_MODEL), win, wout, gf.reshape(1, D_MODEL))


def _fprep_kernel(fl_ref, b_ref, o_ref):
    x = fl_ref[0] + b_ref[...]
    lf = jnp.minimum(x, 0.0) - jnp.log1p(jnp.exp(-jnp.abs(x)))
    lft = lf.T
    r = lax.broadcasted_iota(jnp.int32, (TK, TK), 0)
    c = lax.broadcasted_iota(jnp.int32, (TK, TK), 1)
    u = jnp.where(r <= c, 1.0, 0.0).astype(BF)
    carry = jnp.zeros((N_HEADS, 1), F32)
    for cb in range(SEQ // TK):
        blk = lft[0:N_HEADS, cb * TK:(cb + 1) * TK]
        b0 = blk.astype(BF)
        r1 = blk - b0.astype(F32)
        b1 = r1.astype(BF)
        b2 = (r1 - b1.astype(F32)).astype(BF)
        cs = _dot(b0, u) + _dot(b1, u) + _dot(b2, u) + carry
        o_ref[0, :, cb * TK:(cb + 1) * TK] = cs * LOG2E
        carry = cs[:, TK - 1:TK]


def _fprep_call(flog, bias):
    return pl.pallas_call(
        _fprep_kernel, grid=(BATCH,),
        in_specs=[pl.BlockSpec((1, SEQ, LANES), lambda b: (b, 0, 0)),
                  pl.BlockSpec((1, LANES), lambda b: (0, 0))],
        out_specs=pl.BlockSpec((1, N_HEADS, SEQ), lambda b: (b, 0, 0)),
        out_shape=jax.ShapeDtypeStruct((BATCH, N_HEADS, SEQ), F32),
        compiler_params=pltpu.CompilerParams(dimension_semantics=("parallel",),
                                             vmem_limit_bytes=VMEM_LIMIT),
        name="forget_cumsum")(flog, bias)


def _tri_mask(rows, strict, row0=0):
    row = lax.broadcasted_iota(jnp.int32, (rows, TK), 0) + row0
    col = lax.broadcasted_iota(jnp.int32, (rows, TK), 1)
    return (col < row) if strict else (col <= row)


def _head_split(q):
    lane = lax.broadcasted_iota(jnp.int32, q.shape, 1)
    zero = jnp.zeros_like(q)
    return jnp.where(lane < HEAD_DIM, q, zero), jnp.where(lane >= HEAD_DIM, q, zero)


def _fox_kernel(q_ref, k_ref, v_ref, f_ref, o_ref, va_ref, vb_ref):
    pair = pl.program_id(1)
    lane_kv = lax.broadcasted_iota(jnp.int32, (SEQ, LANES), 1)
    v = v_ref[0]
    one = jnp.ones_like(v)
    va_ref[...] = jnp.where(lane_kv < HEAD_DIM, v, one)
    vb_ref[...] = jnp.where(lane_kv >= HEAD_DIM, v, one)
    vxs = (va_ref, vb_ref)
    diag_ok = _tri_mask(RC, False)
    lane = lax.broadcasted_iota(jnp.int32, (RC, LANES), 1)

    def qk(c):
        qh = _head_split(q_ref[0, c * RC:(c + 1) * RC, :])
        k = k_ref[0, 0:(c + 1) * RC, :]
        return [_dot_nt(qh[hh], k) for hh in (0, 1)]

    def softmax(c, zs):
        kend = (c + 1) * RC
        ps = []
        for hh in (0, 1):
            s = zs[hh] - f_ref[0, pl.ds(2 * pair + hh, 1), 0:kend]
            tail = jnp.where(diag_ok, s[:, kend - RC:], NEG_INF)
            s = tail if c == 0 else jnp.concatenate([s[:, :kend - RC], tail], axis=1)
            ps.append(jnp.exp2(s - jnp.max(s, axis=-1, keepdims=True)).astype(BF))
        return ps

    def pv(c, ps):
        outs = []
        for hh in (0, 1):
            o = _dot(ps[hh], vxs[hh][0:(c + 1) * RC, :])
            outs.append(o / pltpu.roll(o, HEAD_DIM, 1))
        o_ref[0, c * RC:(c + 1) * RC, :] = jnp.where(lane < HEAD_DIM, outs[0], outs[1]).astype(BF)

    n_chunks = SEQ // RC
    zs = qk(0)
    for c in range(n_chunks):
        zs_next = qk(c + 1) if c + 1 < n_chunks else None
        pv(c, softmax(c, zs))
        zs = zs_next


def _fox_call(qkv3, f_row):
    n_pairs = N_HEADS // 2
    return pl.pallas_call(
        _fox_kernel, grid=(BATCH, n_pairs),
        in_specs=[pl.BlockSpec((1, SEQ, LANES), lambda b, p: (b, 0, p)),
                  pl.BlockSpec((1, SEQ, LANES), lambda b, p: (b, 0, n_pairs + p)),
                  pl.BlockSpec((1, SEQ, LANES), lambda b, p: (b, 0, 2 * n_pairs + p)),
                  pl.BlockSpec((1, N_HEADS, SEQ), lambda b, p: (b, 0, 0))],
        out_specs=pl.BlockSpec((1, SEQ, LANES), lambda b, p: (b, 0, p)),
        out_shape=jax.ShapeDtypeStruct((BATCH, SEQ, D_ATTN), BF),
        scratch_shapes=[pltpu.VMEM((SEQ, LANES), BF), pltpu.VMEM((SEQ, LANES), BF)],
        compiler_params=pltpu.CompilerParams(
            dimension_semantics=("parallel", "arbitrary"),
            vmem_limit_bytes=VMEM_LIMIT),
        name="fox_attn")(qkv3, qkv3, qkv3, f_row)


def _sb_kernel(q_ref, k_ref, v_ref, o_ref, u_ref):
    r = lax.broadcasted_iota(jnp.int32, (TK, TK), 0)
    c = lax.broadcasted_iota(jnp.int32, (TK, TK), 1)
    u_ref[...] = jnp.where(r > c, 1.0, 0.0)
    strict = _tri_mask(RC, True)
    lane = lax.broadcasted_iota(jnp.int32, (RC, LANES), 1)
    heads = (0, 1)

    def qk(c):
        qh = _head_split(q_ref[0, c * RC:(c + 1) * RC, :])
        k = k_ref[0, 0:(c + 1) * RC, :]
        return [_dot_nt(qh[hh], k) for hh in heads]

    def gates(c, zs):
        nb = c + 1
        out = []
        for z in zs:
            nz = -z
            lg = jnp.minimum(nz, 0.0) - jnp.log2(1.0 + jnp.exp2(jnp.minimum(z, nz)))
            arg = z + lg
            blocks = [lg[:, j * TK:(j + 1) * TK] for j in range(nb)]
            blocks[-1] = jnp.where(strict, blocks[-1], 0.0)
            carries, run = [None] * nb, None
            for j in reversed(range(nb)):
                carries[j] = run
                rs = jnp.sum(blocks[j], axis=-1, keepdims=True)
                run = rs if run is None else run + rs
            stacked = blocks[0] if nb == 1 else jnp.concatenate(blocks, axis=0)
            out.append((arg, _dot(stacked, u_ref[...]), carries))
        return out

    def values(c, gs):
        nb = c + 1
        outs = []
        for arg, suf, carries in gs:
            ws = []
            for j in range(nb):
                e = suf[j * RC:(j + 1) * RC]
                if carries[j] is not None:
                    e = e + carries[j]
                ws.append(jnp.exp2(arg[:, j * TK:(j + 1) * TK] + e))
            ws[-1] = jnp.where(strict, ws[-1], 0.0)
            a = (ws[0] if nb == 1 else jnp.concatenate(ws, axis=1)).astype(BF)
            outs.append(_dot(a, v_ref[0, 0:nb * TK, :]))
        o_ref[0, c * RC:(c + 1) * RC, :] = jnp.where(lane < HEAD_DIM, outs[0], outs[1]).astype(BF)

    n_chunks = SEQ // RC
    zs = {0: qk(0)}
    gs = {0: gates(0, zs.pop(0))}
    if n_chunks > 1:
        zs[1] = qk(1)
    for c in range(n_chunks):
        if c + 2 < n_chunks:
            zs[c + 2] = qk(c + 2)
        if c + 1 < n_chunks:
            gs[c + 1] = gates(c + 1, zs.pop(c + 1))
        values(c, gs.pop(c))


def _sb_call(qkv3):
    n_pairs = N_HEADS // 2
    n_sb = n_pairs // 2
    return pl.pallas_call(
        _sb_kernel, grid=(BATCH, n_sb),
        in_specs=[pl.BlockSpec((1, SEQ, LANES), lambda b, p: (b, 0, p)),
                  pl.BlockSpec((1, SEQ, LANES), lambda b, p: (b, 0, n_pairs + p)),
                  pl.BlockSpec((1, SEQ, LANES), lambda b, p: (b, 0, 2 * n_pairs + p))],
        out_specs=pl.BlockSpec((1, SEQ, LANES), lambda b, p: (b, 0, p)),
        out_shape=jax.ShapeDtypeStruct((BATCH, SEQ, D_ATTN // 2), BF),
        scratch_shapes=[pltpu.VMEM((TK, TK), F32)],
        compiler_params=pltpu.CompilerParams(
            dimension_semantics=("parallel", "arbitrary"),
            vmem_limit_bytes=VMEM_LIMIT),
        name="stickbreak_attn")(qkv3, qkv3, qkv3)


def _dil_kernel(q_ref, k_ref, v_ref, c_ref, s1_ref, s2_ref, o_ref,
                nat_ref, qp_ref, kp_ref, vp_ref, xo_ref, xl_ref, on_ref):
    lane = lax.broadcasted_iota(jnp.int32, (DB, LANES), 1)
    is_a = lane < HEAD_DIM

    half = ROT_DIM // 2

    def rot(x, rows):
        return (x * c_ref[rows, :] + pltpu.roll(x, LANES - half, 1) * s1_ref[rows, :]
                + pltpu.roll(x, half, 1) * s2_ref[rows, :])

    for t in range(N_DB):
        rows = slice(t * DB, (t + 1) * DB)
        nat_ref[0, rows, :] = rot(q_ref[0, rows, :].astype(F32), rows)
        nat_ref[1, rows, :] = rot(k_ref[0, rows, :].astype(F32), rows)
        nat_ref[2, rows, :] = v_ref[0, rows, :].astype(F32)

    for pi, (window, d) in enumerate(DIL_PATTERNS):
        per_res = SEQ // d // DB
        for t in range(N_DB):
            r, c = divmod(t, per_res)
            src = pl.ds(r + d * DB * c, DB, stride=d) if d > 1 else pl.ds(t * DB, DB)
            dst = slice(t * DB, (t + 1) * DB)
            qp_ref[pi, dst, :] = nat_ref[0, src, :].astype(BF)
            kp_ref[pi, dst, :] = nat_ref[1, src, :].astype(BF)
            vp_ref[pi, dst, :] = nat_ref[2, src, :].astype(BF)

    row = lax.broadcasted_iota(jnp.int32, (DB, DB), 0)
    col = lax.broadcasted_iota(jnp.int32, (DB, DB), 1)
    win_ok = jnp.concatenate([col >= row, col <= row], axis=1)
    cur_ok = col <= row
    ones = jnp.ones((2 * DB, LANES), BF)

    def window(t, pi):
        per_res = SEQ // DIL_PATTERNS[pi][1] // DB
        has_prev = t % per_res != 0
        return slice((t - 1 if has_prev else t) * DB, (t + 1) * DB)

    for g0 in range(0, N_DB, DIL_GROUP):
        items = [(t, pi) for t in range(g0, g0 + DIL_GROUP) for pi in range(len(DIL_PATTERNS))]
        chains = [(t, pi, hh) for t, pi in items for hh in (0, 1)]
        scores = {}
        for t, pi in items:
            qb = qp_ref[pi, t * DB:(t + 1) * DB, :]
            kw = kp_ref[pi, window(t, pi), :]
            qa, qbb = _head_split(qb)
            scores[(t, pi, 0)] = _dot_nt(qa, kw)
            scores[(t, pi, 1)] = _dot_nt(qbb, kw)
        probs, maxes = {}, {}
        for key in chains:
            s = scores.pop(key)
            s = jnp.where(win_ok if s.shape[1] == 2 * DB else cur_ok, s, NEG_INF)
            m = jnp.max(s, axis=-1, keepdims=True)
            probs[key] = jnp.exp2(s - m).astype(BF)
            maxes[key] = m
        nums, dens = {}, {}
        for t, pi, hh in chains:
            p = probs.pop((t, pi, hh))
            nums[(t, pi, hh)] = _dot(p, vp_ref[pi, window(t, pi), :])
            dens[(t, pi, hh)] = _dot(p, ones[:p.shape[1]])
        for t, pi in items:
            ka, kb = (t, pi, 0), (t, pi, 1)
            den = jnp.where(is_a, dens.pop(ka), dens.pop(kb))
            rows = slice(t * DB, (t + 1) * DB)
            xo_ref[pi, rows, :] = jnp.where(is_a, nums.pop(ka), nums.pop(kb)) / den
            xl_ref[pi, rows, :] = jnp.where(is_a, maxes.pop(ka), maxes.pop(kb)) + jnp.log2(den)

    d_max = DIL_PATTERNS[-1][1]
    for r in range(d_max):
        xs, ls = [], []
        for pi, (window_len, d) in enumerate(DIL_PATTERNS):
            step = d_max // d
            start = (r % d) * (SEQ // d) + r // d
            rows = pl.ds(start, DB, stride=step) if step > 1 else pl.ds(start, DB)
            xs.append(xo_ref[pi, rows, :])
            ls.append(xl_ref[pi, rows, :])
        mx = functools.reduce(jnp.maximum, ls)
        es = [jnp.exp2(l - mx) for l in ls]
        inv = 1.0 / functools.reduce(lambda a, b: a + b, es)
        on_ref[pl.ds(r, DB, stride=d_max), :] = functools.reduce(
            lambda a, b: a + b, [(e * inv) * x for e, x in zip(es, xs)])

    o_ref[0] = on_ref[...].astype(BF)


def _dil_call(qkv3, cos_t, sin1_t, sin2_t):
    n_pairs = N_HEADS // 2
    n_sb = n_pairs // 2
    n_pat = len(DIL_PATTERNS)
    tab = pl.BlockSpec((SEQ, LANES), lambda b, p: (0, 0))
    return pl.pallas_call(
        _dil_kernel, grid=(BATCH, n_pairs - n_sb),
        in_specs=[pl.BlockSpec((1, SEQ, LANES), lambda b, p: (b, 0, n_sb + p)),
                  pl.BlockSpec((1, SEQ, LANES), lambda b, p: (b, 0, n_pairs + n_sb + p)),
                  pl.BlockSpec((1, SEQ, LANES), lambda b, p: (b, 0, 2 * n_pairs + n_sb + p)),
                  tab, tab, tab],
        out_specs=pl.BlockSpec((1, SEQ, LANES), lambda b, p: (b, 0, p)),
        out_shape=jax.ShapeDtypeStruct((BATCH, SEQ, D_ATTN // 2), BF),
        scratch_shapes=[pltpu.VMEM((3, SEQ, LANES), F32),
                        pltpu.VMEM((n_pat, SEQ, LANES), BF),
                        pltpu.VMEM((n_pat, SEQ, LANES), BF),
                        pltpu.VMEM((n_pat, SEQ, LANES), BF),
                        pltpu.VMEM((n_pat, SEQ, LANES), F32),
                        pltpu.VMEM((n_pat, SEQ, LANES), F32),
                        pltpu.VMEM((SEQ, LANES), F32)],
        compiler_params=pltpu.CompilerParams(dimension_semantics=("parallel", "arbitrary"),
                                             vmem_limit_bytes=VMEM_LIMIT),
        name="dilated_attn")(qkv3, qkv3, qkv3, cos_t, sin1_t, sin2_t)


def _rotary_tables():
    half = ROT_DIM // 2
    pos = jnp.arange(SEQ, dtype=F32)
    inv_freq = ROPE_THETA ** (-jnp.arange(half, dtype=F32) * 2.0 / ROT_DIM)
    ang = pos[:, None] * inv_freq[None, :]
    cos, sin = jnp.cos(ang), jnp.sin(ang)
    zeros = jnp.zeros((SEQ, HEAD_DIM - ROT_DIM), F32)
    z8 = jnp.zeros((SEQ, half), F32)
    c_head = jnp.concatenate([cos, cos, zeros + 1.0], axis=1)
    s1_head = jnp.concatenate([-sin, z8, zeros], axis=1)
    s2_head = jnp.concatenate([z8, sin, zeros], axis=1)
    two = lambda t: jnp.concatenate([t, t], axis=1)
    return two(c_head), two(s1_head), two(s2_head)


def kernel(x, norm_mix, w_qkv_even, w_o_even, w_qkvf_odd, b_forget, w_o_odd, norm_ffn,
           w_ffn_in, w_ffn_out, norm_final):
    h = x.reshape(M_TOKENS, D_MODEL)
    cos_t, sin1_t, sin2_t = _rotary_tables()
    n_qkv = 3 * D_ATTN
    for layer in range(DEPTH):
        i = layer // 2
        if layer % 2 == 0:
            qkv = _qkv_call(h, norm_mix[layer], w_qkv_even[i].astype(BF))
            qkv3 = qkv.reshape(BATCH, SEQ, n_qkv)
            oa = _sb_call(qkv3).reshape(M_TOKENS, D_ATTN // 2)
            ob = _dil_call(qkv3, cos_t, sin1_t, sin2_t).reshape(M_TOKENS, D_ATTN // 2)
            ob_col = 0
            wo = w_o_even[i]
        else:
            w = w_qkvf_odd[i]
            wf = jnp.pad(w[:, n_qkv:], ((0, 0), (0, LANES - N_HEADS))).astype(BF)
            qkv, flog = _qkv_call(h, norm_mix[layer], w[:, :n_qkv].astype(BF), wf)
            bias = jnp.pad(b_forget[i], (0, LANES - N_HEADS)).reshape(1, LANES)
            f_row = _fprep_call(flog.reshape(BATCH, SEQ, LANES), bias)
            o = _fox_call(qkv.reshape(BATCH, SEQ, n_qkv), f_row).reshape(M_TOKENS, D_ATTN)
            oa, ob, ob_col = o, o, 1
            wo = w_o_odd[i]
        h = _ffn_call(h, oa, ob, ob_col, wo.astype(BF), norm_ffn[layer],
                      w_ffn_in[layer].astype(BF), w_ffn_out[layer].astype(BF), norm_final,
                      final_norm=(layer == DEPTH - 1))
    return h.reshape(BATCH, SEQ, D_MODEL)
```

```python
import functools
import math

import jax
import jax.numpy as jnp
from jax import lax
from jax.experimental import pallas as pl
from jax.experimental.pallas import tpu as pltpu

D_MODEL = 1024
BATCH = 8
SEQ = 2048
DEPTH = 4
HEAD_DIM = 64
N_HEADS = 16
D_ATTN = N_HEADS * HEAD_DIM
D_FF = 2816
ROPE_THETA = 500000.0
ROT_DIM = HEAD_DIM // 4
DIL_PATTERNS = ((128, 1), (512, 4), (2048, 16))
RMS_EPS = 1e-5
LOG2E = math.log2(math.e)
Q_SCALE = HEAD_DIM ** -0.5 * LOG2E

LANES = 128
M_TOKENS = BATCH * SEQ
TM = 512
TM_FFN = 1024
TN = 1024
TF = 512
TQ = 1024
TK = 256
RC = 256
DB = 128
N_DB = SEQ // DB
DIL_GROUP = 8
FOX_AHEAD = 2
VMEM_LIMIT = 56 * 1024 * 1024

BF = jnp.bfloat16
F32 = jnp.float32
NEG_INF = float("-inf")


def _rms(x, g):
    ms = jnp.mean(x * x, axis=-1, keepdims=True)
    return x * lax.rsqrt(ms + RMS_EPS) * g


def _dot(a, b):
    return jnp.dot(a, b, preferred_element_type=F32)


def _dot_nt(a, b):
    return lax.dot_general(a, b, (((1,), (1,)), ((), ())), preferred_element_type=F32)


def _set_rows(full, r0, part):
    return part if r0 == 0 else jnp.concatenate([full[:r0], part], axis=0)


def _qkv_kernel(x_ref, g_ref, w_ref, *rest):
    hn = _rms(x_ref[...], g_ref[...]).astype(BF)
    o_ref = rest[-1] if len(rest) == 1 else rest[1]
    for j in range(3 * D_ATTN // TN):
        y = _dot(hn, w_ref[:, j * TN:(j + 1) * TN])
        if (j + 1) * TN <= D_ATTN:
            y = y * Q_SCALE
        o_ref[:, j * TN:(j + 1) * TN] = y.astype(BF)
    if len(rest) == 3:
        wf_ref, _, f_ref = rest
        f_ref[...] = _dot(hn, wf_ref[...])


def _const_spec(shape, layer=None):
    if layer is None:
        return pl.BlockSpec(shape, lambda *_: (0,) * len(shape), pipeline_mode=pl.Buffered(1))
    return pl.BlockSpec((None,) + shape, lambda *_: (layer,) + (0,) * len(shape),
                        pipeline_mode=pl.Buffered(1))


def _qkv_call(x, g, w, wf=None, layer=None):
    n = 3 * D_ATTN
    in_specs = [pl.BlockSpec((TM, D_MODEL), lambda i: (i, 0)),
                _const_spec((1, D_MODEL)),
                _const_spec((D_MODEL, n), layer)]
    out_specs = pl.BlockSpec((TM, n), lambda i: (i, 0))
    out_shape = jax.ShapeDtypeStruct((M_TOKENS, n), BF)
    args = [x, g.reshape(1, D_MODEL), w]
    if wf is not None:
        in_specs.append(_const_spec((D_MODEL, LANES)))
        out_specs = [out_specs, pl.BlockSpec((TM, LANES), lambda i: (i, 0))]
        out_shape = [out_shape, jax.ShapeDtypeStruct((M_TOKENS, LANES), F32)]
        args.append(wf)
    return pl.pallas_call(
        _qkv_kernel, grid=(M_TOKENS // TM,), in_specs=in_specs, out_specs=out_specs,
        out_shape=out_shape,
        compiler_params=pltpu.CompilerParams(dimension_semantics=("parallel",),
                                             vmem_limit_bytes=VMEM_LIMIT),
        name="norm_qkv" if wf is None else "norm_qkvf")(*args)


def _ffn_kernel(x_ref, oa_ref, ob_ref, wo_ref, g_ref, win_ref, wout_ref, gf_ref, out_ref,
                acc_ref, hn_ref, *, final_norm):
    half = D_ATTN // 2
    xn = x_ref[...] + _dot(oa_ref[...], wo_ref[0:half, :]) + _dot(ob_ref[...], wo_ref[half:, :])
    acc_ref[...] = xn
    hn_ref[...] = _rms(xn, g_ref[...]).astype(BF)

    def tile(c0, width):
        hn = hn_ref[...]
        c1 = D_FF + c0
        if not isinstance(c0, int):
            c1 = pl.multiple_of(c1, math.gcd(D_FF, TF))
        g = _dot(hn, win_ref[:, pl.ds(c0, width)])
        u = _dot(hn, win_ref[:, pl.ds(c1, width)])
        a = (g * (1.0 / (1.0 + jnp.exp(-g))) * u).astype(BF)
        acc_ref[...] += _dot(a, wout_ref[pl.ds(c0, width), :])

    def body(f, c):
        tile(pl.multiple_of(f * TF, TF), TF)
        return c

    n_tiles = D_FF // TF
    lax.fori_loop(0, n_tiles, body, 0)
    if D_FF > n_tiles * TF:
        tile(n_tiles * TF, D_FF - n_tiles * TF)
    y = acc_ref[...]
    if final_norm:
        y = _rms(y, gf_ref[...])
    out_ref[...] = y


def _ffn_call(x, oa, ob, ob_col, wo, wo_layer, g, win, wout, layer, gf, final_norm):
    half = D_ATTN // 2
    in_specs = [pl.BlockSpec((TM_FFN, D_MODEL), lambda i: (i, 0)),
                pl.BlockSpec((TM_FFN, half), lambda i: (i, 0)),
                pl.BlockSpec((TM_FFN, half), lambda i: (i, ob_col)),
                _const_spec((D_ATTN, D_MODEL), wo_layer),
                _const_spec((1, D_MODEL)),
                _const_spec((D_MODEL, 2 * D_FF), layer),
                _const_spec((D_FF, D_MODEL), layer),
                _const_spec((1, D_MODEL))]
    return pl.pallas_call(
        functools.partial(_ffn_kernel, final_norm=final_norm),
        grid=(M_TOKENS // TM_FFN,), in_specs=in_specs,
        out_specs=pl.BlockSpec((TM_FFN, D_MODEL), lambda i: (i, 0)),
        out_shape=jax.ShapeDtypeStruct((M_TOKENS, D_MODEL), F32),
        scratch_shapes=[pltpu.VMEM((TM_FFN, D_MODEL), F32), pltpu.VMEM((TM_FFN, D_MODEL), BF)],
        compiler_params=pltpu.CompilerParams(dimension_semantics=("parallel",),
                                             vmem_limit_bytes=VMEM_LIMIT),
        name="oproj_ffn")(x, oa, ob, wo, g.reshape(1, D_MODEL), win, wout, gf.reshape(1, D_MODEL))


def _fprep_kernel(fl_ref, b_ref, o_ref):
    x = fl_ref[0] + b_ref[...]
    lf = jnp.minimum(x, 0.0) - jnp.log1p(jnp.exp(-jnp.abs(x)))
    lft = lf.T
    r = lax.broadcasted_iota(jnp.int32, (TK, TK), 0)
    c = lax.broadcasted_iota(jnp.int32, (TK, TK), 1)
    u = jnp.where(r <= c, 1.0, 0.0).astype(BF)
    carry = jnp.zeros((N_HEADS, 1), F32)
    for cb in range(SEQ // TK):
        blk = lft[0:N_HEADS, cb * TK:(cb + 1) * TK]
        b0 = blk.astype(BF)
        r1 = blk - b0.astype(F32)
        b1 = r1.astype(BF)
        b2 = (r1 - b1.astype(F32)).astype(BF)
        cs = _dot(b0, u) + _dot(b1, u) + _dot(b2, u) + carry
        o_ref[0, :, cb * TK:(cb + 1) * TK] = cs * LOG2E
        carry = cs[:, TK - 1:TK]


def _fprep_call(flog, bias):
    return pl.pallas_call(
        _fprep_kernel, grid=(BATCH,),
        in_specs=[pl.BlockSpec((1, SEQ, LANES), lambda b: (b, 0, 0)),
                  pl.BlockSpec((1, LANES), lambda b: (0, 0))],
        out_specs=pl.BlockSpec((1, N_HEADS, SEQ), lambda b: (b, 0, 0)),
        out_shape=jax.ShapeDtypeStruct((BATCH, N_HEADS, SEQ), F32),
        compiler_params=pltpu.CompilerParams(dimension_semantics=("parallel",),
                                             vmem_limit_bytes=VMEM_LIMIT),
        name="forget_cumsum")(flog, bias)


def _tri_mask(rows, strict, row0=0):
    row = lax.broadcasted_iota(jnp.int32, (rows, TK), 0) + row0
    col = lax.broadcasted_iota(jnp.int32, (rows, TK), 1)
    return (col < row) if strict else (col <= row)


def _head_split(q):
    lane = lax.broadcasted_iota(jnp.int32, q.shape, 1)
    zero = jnp.zeros_like(q)
    return jnp.where(lane < HEAD_DIM, q, zero), jnp.where(lane >= HEAD_DIM, q, zero)


def _fox_kernel(q_ref, k_ref, v_ref, f_ref, o_ref, va_ref, vb_ref):
    pair = pl.program_id(1)
    lane_kv = lax.broadcasted_iota(jnp.int32, (SEQ, LANES), 1)
    v = v_ref[0]
    one = jnp.ones_like(v)
    va_ref[...] = jnp.where(lane_kv < HEAD_DIM, v, one)
    vb_ref[...] = jnp.where(lane_kv >= HEAD_DIM, v, one)
    vxs = (va_ref, vb_ref)
    diag_ok = _tri_mask(RC, False)
    is_a = lax.broadcasted_iota(jnp.int32, (RC, LANES), 1) < HEAD_DIM

    def qk(c):
        qh = _head_split(q_ref[0, c * RC:(c + 1) * RC, :])
        k = k_ref[0, 0:(c + 1) * RC, :]
        return [_dot_nt(qh[hh], k) for hh in (0, 1)]

    def softmax(c, zs):
        kend = (c + 1) * RC
        ps = []
        for hh in (0, 1):
            s = zs[hh] - f_ref[0, pl.ds(2 * pair + hh, 1), 0:kend]
            tail = jnp.where(diag_ok, s[:, kend - RC:], NEG_INF)
            s = tail if c == 0 else jnp.concatenate([s[:, :kend - RC], tail], axis=1)
            ps.append(jnp.exp2(s - jnp.max(s, axis=-1, keepdims=True)).astype(BF))
        return ps

    def pv(c, ps):
        outs = []
        for hh in (0, 1):
            o = _dot(ps[hh], vxs[hh][0:(c + 1) * RC, :])
            outs.append(o / pltpu.roll(o, HEAD_DIM, 1))
        o_ref[0, c * RC:(c + 1) * RC, :] = jnp.where(is_a, outs[0], outs[1]).astype(BF)

    n_chunks = SEQ // RC
    zs = {c: qk(c) for c in range(min(FOX_AHEAD, n_chunks))}
    for c in range(n_chunks):
        if c + FOX_AHEAD < n_chunks:
            zs[c + FOX_AHEAD] = qk(c + FOX_AHEAD)
        pv(c, softmax(c, zs.pop(c)))


def _fox_call(qkv3, f_row):
    n_pairs = N_HEADS // 2
    return pl.pallas_call(
        _fox_kernel, grid=(BATCH, n_pairs),
        in_specs=[pl.BlockSpec((1, SEQ, LANES), lambda b, p: (b, 0, p)),
                  pl.BlockSpec((1, SEQ, LANES), lambda b, p: (b, 0, n_pairs + p)),
                  pl.BlockSpec((1, SEQ, LANES), lambda b, p: (b, 0, 2 * n_pairs + p)),
                  pl.BlockSpec((1, N_HEADS, SEQ), lambda b, p: (b, 0, 0))],
        out_specs=pl.BlockSpec((1, SEQ, LANES), lambda b, p: (b, 0, p)),
        out_shape=jax.ShapeDtypeStruct((BATCH, SEQ, D_ATTN), BF),
        scratch_shapes=[pltpu.VMEM((SEQ, LANES), BF), pltpu.VMEM((SEQ, LANES), BF)],
        compiler_params=pltpu.CompilerParams(
            dimension_semantics=("parallel", "arbitrary"),
            vmem_limit_bytes=VMEM_LIMIT),
        name="fox_attn")(qkv3, qkv3, qkv3, f_row)


def _sb_kernel(q_ref, k_ref, v_ref, o_ref, u_ref):
    r = lax.broadcasted_iota(jnp.int32, (TK, TK), 0)
    c = lax.broadcasted_iota(jnp.int32, (TK, TK), 1)
    u_ref[...] = jnp.where(r > c, 1.0, 0.0)
    strict = _tri_mask(RC, True)
    lane = lax.broadcasted_iota(jnp.int32, (RC, LANES), 1)
    heads = (0, 1)

    def qk(c):
        qh = _head_split(q_ref[0, c * RC:(c + 1) * RC, :])
        z = _dot_nt(jnp.concatenate(qh, axis=0), k_ref[0, 0:(c + 1) * RC, :])
        return z[:RC], z[RC:]

    def gates(c, zs):
        nb = c + 1
        parts = []
        for z in zs:
            nz = -z
            lg = jnp.minimum(nz, 0.0) - jnp.log2(1.0 + jnp.exp2(jnp.minimum(z, nz)))
            arg = z + lg
            blocks = [lg[:, j * TK:(j + 1) * TK] for j in range(nb)]
            blocks[-1] = jnp.where(strict, blocks[-1], 0.0)
            carries, run = [None] * nb, None
            for j in reversed(range(nb)):
                carries[j] = run
                rs = jnp.sum(blocks[j], axis=-1, keepdims=True)
                run = rs if run is None else run + rs
            parts.append((arg, blocks, carries))
        suf = _dot(jnp.concatenate(parts[0][1] + parts[1][1], axis=0), u_ref[...])
        half = nb * RC
        return [(parts[hh][0], suf[hh * half:(hh + 1) * half], parts[hh][2]) for hh in heads]

    def values(c, gs):
        nb = c + 1
        weights = []
        for arg, suf, carries in gs:
            ws = []
            for j in range(nb):
                e = suf[j * RC:(j + 1) * RC]
                if carries[j] is not None:
                    e = e + carries[j]
                ws.append(jnp.exp2(arg[:, j * TK:(j + 1) * TK] + e))
            ws[-1] = jnp.where(strict, ws[-1], 0.0)
            weights.append((ws[0] if nb == 1 else jnp.concatenate(ws, axis=1)).astype(BF))
        o = _dot(jnp.concatenate(weights, axis=0), v_ref[0, 0:nb * TK, :])
        o_ref[0, c * RC:(c + 1) * RC, :] = jnp.where(lane < HEAD_DIM, o[:RC], o[RC:]).astype(BF)

    n_chunks = SEQ // RC
    zs = {0: qk(0)}
    gs = {0: gates(0, zs.pop(0))}
    if n_chunks > 1:
        zs[1] = qk(1)
    for c in range(n_chunks):
        if c + 2 < n_chunks:
            zs[c + 2] = qk(c + 2)
        if c + 1 < n_chunks:
            gs[c + 1] = gates(c + 1, zs.pop(c + 1))
        values(c, gs.pop(c))


def _sb_call(qkv3):
    n_pairs = N_HEADS // 2
    n_sb = n_pairs // 2
    return pl.pallas_call(
        _sb_kernel, grid=(BATCH, n_sb),
        in_specs=[pl.BlockSpec((1, SEQ, LANES), lambda b, p: (b, 0, p)),
                  pl.BlockSpec((1, SEQ, LANES), lambda b, p: (b, 0, n_pairs + p)),
                  pl.BlockSpec((1, SEQ, LANES), lambda b, p: (b, 0, 2 * n_pairs + p))],
        out_specs=pl.BlockSpec((1, SEQ, LANES), lambda b, p: (b, 0, p)),
        out_shape=jax.ShapeDtypeStruct((BATCH, SEQ, D_ATTN // 2), BF),
        scratch_shapes=[pltpu.VMEM((TK, TK), F32)],
        compiler_params=pltpu.CompilerParams(
            dimension_semantics=("parallel", "arbitrary"),
            vmem_limit_bytes=VMEM_LIMIT),
        name="stickbreak_attn")(qkv3, qkv3, qkv3)


def _dil_kernel(q_ref, k_ref, v_ref, c_ref, s1_ref, s2_ref, o_ref,
                nat_ref, qp_ref, kp_ref, vp_ref, xo_ref, xl_ref, on_ref):
    lane = lax.broadcasted_iota(jnp.int32, (DB, LANES), 1)
    is_a = lane < HEAD_DIM

    half = ROT_DIM // 2

    def rot(x, rows):
        return (x * c_ref[rows, :] + pltpu.roll(x, LANES - half, 1) * s1_ref[rows, :]
                + pltpu.roll(x, half, 1) * s2_ref[rows, :])

    for t in range(N_DB):
        rows = slice(t * DB, (t + 1) * DB)
        nat_ref[0, rows, :] = rot(q_ref[0, rows, :].astype(F32), rows)
        nat_ref[1, rows, :] = rot(k_ref[0, rows, :].astype(F32), rows)
        nat_ref[2, rows, :] = v_ref[0, rows, :].astype(F32)

    for pi, (window, d) in enumerate(DIL_PATTERNS):
        per_res = SEQ // d // DB
        for t in range(N_DB):
            r, c = divmod(t, per_res)
            src = pl.ds(r + d * DB * c, DB, stride=d) if d > 1 else pl.ds(t * DB, DB)
            dst = slice(t * DB, (t + 1) * DB)
            qp_ref[pi, dst, :] = nat_ref[0, src, :].astype(BF)
            kp_ref[pi, dst, :] = nat_ref[1, src, :].astype(BF)
            vp_ref[pi, dst, :] = nat_ref[2, src, :].astype(BF)

    row = lax.broadcasted_iota(jnp.int32, (DB, DB), 0)
    col = lax.broadcasted_iota(jnp.int32, (DB, DB), 1)
    win_ok = jnp.concatenate([col >= row, col <= row], axis=1)
    cur_ok = col <= row
    ones = jnp.ones((2 * DB, LANES), BF)

    def window(t, pi):
        per_res = SEQ // DIL_PATTERNS[pi][1] // DB
        has_prev = t % per_res != 0
        return slice((t - 1 if has_prev else t) * DB, (t + 1) * DB)

    for g0 in range(0, N_DB, DIL_GROUP):
        items = [(t, pi) for t in range(g0, g0 + DIL_GROUP) for pi in range(len(DIL_PATTERNS))]
        chains = [(t, pi, hh) for t, pi in items for hh in (0, 1)]
        scores = {}
        for t, pi in items:
            qb = qp_ref[pi, t * DB:(t + 1) * DB, :]
            kw = kp_ref[pi, window(t, pi), :]
            qa, qbb = _head_split(qb)
            scores[(t, pi, 0)] = _dot_nt(qa, kw)
            scores[(t, pi, 1)] = _dot_nt(qbb, kw)
        probs, maxes = {}, {}
        for key in chains:
            s = scores.pop(key)
            s = jnp.where(win_ok if s.shape[1] == 2 * DB else cur_ok, s, NEG_INF)
            m = jnp.max(s, axis=-1, keepdims=True)
            probs[key] = jnp.exp2(s - m).astype(BF)
            maxes[key] = m
        nums, dens = {}, {}
        for t, pi, hh in chains:
            p = probs.pop((t, pi, hh))
            nums[(t, pi, hh)] = _dot(p, vp_ref[pi, window(t, pi), :])
            dens[(t, pi, hh)] = _dot(p, ones[:p.shape[1]])
        for t, pi in items:
            ka, kb = (t, pi, 0), (t, pi, 1)
            den = jnp.where(is_a, dens.pop(ka), dens.pop(kb))
            rows = slice(t * DB, (t + 1) * DB)
            xo_ref[pi, rows, :] = jnp.where(is_a, nums.pop(ka), nums.pop(kb)) / den
            xl_ref[pi, rows, :] = jnp.where(is_a, maxes.pop(ka), maxes.pop(kb)) + jnp.log2(den)

    d_max = DIL_PATTERNS[-1][1]
    for r in range(d_max):
        xs, ls = [], []
        for pi, (window_len, d) in enumerate(DIL_PATTERNS):
            step = d_max // d
            start = (r % d) * (SEQ // d) + r // d
            rows = pl.ds(start, DB, stride=step) if step > 1 else pl.ds(start, DB)
            xs.append(xo_ref[pi, rows, :])
            ls.append(xl_ref[pi, rows, :])
        mx = functools.reduce(jnp.maximum, ls)
        es = [jnp.exp2(l - mx) for l in ls]
        inv = 1.0 / functools.reduce(lambda a, b: a + b, es)
        on_ref[pl.ds(r, DB, stride=d_max), :] = functools.reduce(
            lambda a, b: a + b, [(e * inv) * x for e, x in zip(es, xs)])

    o_ref[0] = on_ref[...].astype(BF)


def _dil_call(qkv3, cos_t, sin1_t, sin2_t):
    n_pairs = N_HEADS // 2
    n_sb = n_pairs // 2
    n_pat = len(DIL_PATTERNS)
    tab = pl.BlockSpec((SEQ, LANES), lambda b, p: (0, 0))
    return pl.pallas_call(
        _dil_kernel, grid=(BATCH, n_pairs - n_sb),
        in_specs=[pl.BlockSpec((1, SEQ, LANES), lambda b, p: (b, 0, n_sb + p)),
                  pl.BlockSpec((1, SEQ, LANES), lambda b, p: (b, 0, n_pairs + n_sb + p)),
                  pl.BlockSpec((1, SEQ, LANES), lambda b, p: (b, 0, 2 * n_pairs + n_sb + p)),
                  tab, tab, tab],
        out_specs=pl.BlockSpec((1, SEQ, LANES), lambda b, p: (b, 0, p)),
        out_shape=jax.ShapeDtypeStruct((BATCH, SEQ, D_ATTN // 2), BF),
        scratch_shapes=[pltpu.VMEM((3, SEQ, LANES), F32),
                        pltpu.VMEM((n_pat, SEQ, LANES), BF),
                        pltpu.VMEM((n_pat, SEQ, LANES), BF),
                        pltpu.VMEM((n_pat, SEQ, LANES), BF),
                        pltpu.VMEM((n_pat, SEQ, LANES), F32),
                        pltpu.VMEM((n_pat, SEQ, LANES), F32),
                        pltpu.VMEM((SEQ, LANES), F32)],
        compiler_params=pltpu.CompilerParams(dimension_semantics=("parallel", "arbitrary"),
                                             vmem_limit_bytes=VMEM_LIMIT),
        name="dilated_attn")(qkv3, qkv3, qkv3, cos_t, sin1_t, sin2_t)


def _rotary_tables():
    half = ROT_DIM // 2
    pos = jnp.arange(SEQ, dtype=F32)
    inv_freq = ROPE_THETA ** (-jnp.arange(half, dtype=F32) * 2.0 / ROT_DIM)
    ang = pos[:, None] * inv_freq[None, :]
    cos, sin = jnp.cos(ang), jnp.sin(ang)
    zeros = jnp.zeros((SEQ, HEAD_DIM - ROT_DIM), F32)
    z8 = jnp.zeros((SEQ, half), F32)
    c_head = jnp.concatenate([cos, cos, zeros + 1.0], axis=1)
    s1_head = jnp.concatenate([-sin, z8, zeros], axis=1)
    s2_head = jnp.concatenate([z8, sin, zeros], axis=1)
    two = lambda t: jnp.concatenate([t, t], axis=1)
    return two(c_head), two(s1_head), two(s2_head)


def kernel(x, norm_mix, w_qkv_even, w_o_even, w_qkvf_odd, b_forget, w_o_odd, norm_ffn,
           w_ffn_in, w_ffn_out, norm_final):
    h = x.reshape(M_TOKENS, D_MODEL)
    cos_t, sin1_t, sin2_t = _rotary_tables()
    n_qkv = 3 * D_ATTN
    w_qkv_even, w_o_even, w_o_odd = (t.astype(BF) for t in (w_qkv_even, w_o_even, w_o_odd))
    w_ffn_in, w_ffn_out = w_ffn_in.astype(BF), w_ffn_out.astype(BF)
    for layer in range(DEPTH):
        i = layer // 2
        if layer % 2 == 0:
            qkv = _qkv_call(h, norm_mix[layer], w_qkv_even, layer=i)
            qkv3 = qkv.reshape(BATCH, SEQ, n_qkv)
            oa = _sb_call(qkv3).reshape(M_TOKENS, D_ATTN // 2)
            ob = _dil_call(qkv3, cos_t, sin1_t, sin2_t).reshape(M_TOKENS, D_ATTN // 2)
            ob_col = 0
            wo = w_o_even
        else:
            w = w_qkvf_odd[i]
            wf = jnp.pad(w[:, n_qkv:], ((0, 0), (0, LANES - N_HEADS))).astype(BF)
            qkv, flog = _qkv_call(h, norm_mix[layer], w[:, :n_qkv].astype(BF), wf)
            bias = jnp.pad(b_forget[i], (0, LANES - N_HEADS)).reshape(1, LANES)
            f_row = _fprep_call(flog.reshape(BATCH, SEQ, LANES), bias)
            o = _fox_call(qkv.reshape(BATCH, SEQ, n_qkv), f_row).reshape(M_TOKENS, D_ATTN)
            oa, ob, ob_col = o, o, 1
            wo = w_o_odd
        h = _ffn_call(h, oa, ob, ob_col, wo, i, norm_ffn[layer], w_ffn_in, w_ffn_out, layer,
                      norm_final, final_norm=(layer == DEPTH - 1))
    return h.reshape(BATCH, SEQ, D_MODEL)
```

```python
import functools
import math

import jax
import jax.numpy as jnp
from jax import lax
from jax.experimental import pallas as pl
from jax.experimental.pallas import tpu as pltpu

D_MODEL = 1024
BATCH = 8
SEQ = 2048
DEPTH = 4
HEAD_DIM = 64
N_HEADS = 16
D_ATTN = N_HEADS * HEAD_DIM
D_FF = 2816
ROPE_THETA = 500000.0
ROT_DIM = HEAD_DIM // 4
DIL_PATTERNS = ((128, 1), (512, 4), (2048, 16))
RMS_EPS = 1e-5
LOG2E = math.log2(math.e)
Q_SCALE = HEAD_DIM ** -0.5 * LOG2E

LANES = 128
M_TOKENS = BATCH * SEQ
TM = 512
TM_FFN = 1024
TN = 1024
TF = 512
TQ = 1024
TK = 256
RC = 256
DB = 128
N_DB = SEQ // DB
DIL_GROUP = 8
FOX_AHEAD = 1
VMEM_LIMIT = 56 * 1024 * 1024

BF = jnp.bfloat16
F32 = jnp.float32
NEG_INF = float("-inf")


def _rms(x, g):
    ms = jnp.mean(x * x, axis=-1, keepdims=True)
    return x * lax.rsqrt(ms + RMS_EPS) * g


def _dot(a, b):
    return jnp.dot(a, b, preferred_element_type=F32)


def _dot_nt(a, b):
    return lax.dot_general(a, b, (((1,), (1,)), ((), ())), preferred_element_type=F32)


def _set_rows(full, r0, part):
    return part if r0 == 0 else jnp.concatenate([full[:r0], part], axis=0)


def _rotate(x, c, s1, s2):
    half = ROT_DIM // 2
    return x * c + pltpu.roll(x, LANES - half, 1) * s1 + pltpu.roll(x, half, 1) * s2


def _qkv_kernel(x_ref, g_ref, w_ref, *rest, has_forget, rotary, n_cast):
    n_extra = int(has_forget) + 3 * int(rotary)
    ins, outs = rest[:n_extra + n_cast], rest[n_extra + n_cast:]
    hn = _rms(x_ref[...], g_ref[...]).astype(BF)
    o_ref = outs[0]
    n_tiles = D_ATTN // LANES
    for j in range(3 * D_ATTN // TN):
        y = _dot(hn, w_ref[:, j * TN:(j + 1) * TN])
        if (j + 1) * TN <= D_ATTN:
            y = y * Q_SCALE
        if rotary and (j + 1) * TN <= 2 * D_ATTN:
            c, s1, s2 = (r[...] for r in ins[int(has_forget):n_extra])
            tiles = [y[:, t * LANES:(t + 1) * LANES] for t in range(n_tiles)]
            tiles = [x if t < n_tiles // 2 else _rotate(x, c, s1, s2) for t, x in enumerate(tiles)]
            y = jnp.concatenate(tiles, axis=1)
        o_ref[:, j * TN:(j + 1) * TN] = y.astype(BF)
    if has_forget:
        outs[1][...] = _dot(hn, ins[0][...])
    for src, dst in zip(ins[n_extra:], outs[1 + int(has_forget):]):
        dst[...] = src[...].astype(BF)


def _const_spec(shape, layer=None):
    if layer is None:
        return pl.BlockSpec(shape, lambda *_: (0,) * len(shape), pipeline_mode=pl.Buffered(1))
    return pl.BlockSpec((None,) + shape, lambda *_: (layer,) + (0,) * len(shape),
                        pipeline_mode=pl.Buffered(1))


def _qkv_call(x, g, w, wf=None, layer=None, rot_tables=None, casts=()):
    n = 3 * D_ATTN
    steps = M_TOKENS // TM
    in_specs = [pl.BlockSpec((TM, D_MODEL), lambda i: (i, 0)),
                _const_spec((1, D_MODEL)),
                _const_spec((D_MODEL, n), layer)]
    out_specs = [pl.BlockSpec((TM, n), lambda i: (i, 0))]
    out_shape = [jax.ShapeDtypeStruct((M_TOKENS, n), BF)]
    args = [x, g.reshape(1, D_MODEL), w]
    if wf is not None:
        in_specs.append(_const_spec((D_MODEL, LANES)))
        out_specs.append(pl.BlockSpec((TM, LANES), lambda i: (i, 0)))
        out_shape.append(jax.ShapeDtypeStruct((M_TOKENS, LANES), F32))
        args.append(wf)
    if rot_tables is not None:
        for tab in rot_tables:
            in_specs.append(pl.BlockSpec((TM, LANES), lambda i: (i % (SEQ // TM), 0)))
            args.append(tab)
    for stack, idx in casts:
        slab = stack[0].size // (steps * LANES)
        in_specs.append(pl.BlockSpec((None, None, slab, LANES), lambda i, idx=idx: (idx, i, 0, 0)))
        out_specs.append(pl.BlockSpec((None, slab, LANES), lambda i: (i, 0, 0)))
        out_shape.append(jax.ShapeDtypeStruct((steps, slab, LANES), BF))
        args.append(stack.reshape(stack.shape[0], steps, slab, LANES))
    outs = pl.pallas_call(
        functools.partial(_qkv_kernel, has_forget=wf is not None, rotary=rot_tables is not None,
                          n_cast=len(casts)),
        grid=(steps,), in_specs=in_specs, out_specs=out_specs, out_shape=out_shape,
        compiler_params=pltpu.CompilerParams(dimension_semantics=("parallel",),
                                             vmem_limit_bytes=VMEM_LIMIT),
        name="norm_qkv" if wf is None else "norm_qkvf")(*args)
    n_proj = len(outs) - len(casts)
    cast_outs = [o.reshape(stack.shape[1:]) for o, (stack, _) in zip(outs[n_proj:], casts)]
    return list(outs[:n_proj]) + cast_outs


def _ffn_kernel(x_ref, oa_ref, ob_ref, wo_ref, g_ref, win_ref, wout_ref, gf_ref, out_ref,
                acc_ref, hn_ref, *, final_norm):
    half = D_ATTN // 2
    xn = x_ref[...] + _dot(oa_ref[...], wo_ref[0:half, :]) + _dot(ob_ref[...], wo_ref[half:, :])
    acc_ref[...] = xn
    hn_ref[...] = _rms(xn, g_ref[...]).astype(BF)

    def tile(c0, width):
        hn = hn_ref[...]
        c1 = D_FF + c0
        if not isinstance(c0, int):
            c1 = pl.multiple_of(c1, math.gcd(D_FF, TF))
        g = _dot(hn, win_ref[:, pl.ds(c0, width)])
        u = _dot(hn, win_ref[:, pl.ds(c1, width)])
        a = (g * (1.0 / (1.0 + jnp.exp(-g))) * u).astype(BF)
        acc_ref[...] += _dot(a, wout_ref[pl.ds(c0, width), :])

    def body(f, c):
        tile(pl.multiple_of(f * TF, TF), TF)
        return c

    n_tiles = D_FF // TF
    lax.fori_loop(0, n_tiles, body, 0)
    if D_FF > n_tiles * TF:
        tile(n_tiles * TF, D_FF - n_tiles * TF)
    y = acc_ref[...]
    if final_norm:
        y = _rms(y, gf_ref[...])
    out_ref[...] = y


def _ffn_call(x, oa, ob, ob_col, wo, g, win, wout, gf, final_norm):
    half = D_ATTN // 2
    in_specs = [pl.BlockSpec((TM_FFN, D_MODEL), lambda i: (i, 0)),
                pl.BlockSpec((TM_FFN, half), lambda i: (i, 0)),
                pl.BlockSpec((TM_FFN, half), lambda i: (i, ob_col)),
                _const_spec((D_ATTN, D_MODEL)),
                _const_spec((1, D_MODEL)),
                _const_spec((D_MODEL, 2 * D_FF)),
                _const_spec((D_FF, D_MODEL)),
                _const_spec((1, D_MODEL))]
    return pl.pallas_call(
        functools.partial(_ffn_kernel, final_norm=final_norm),
        grid=(M_TOKENS // TM_FFN,), in_specs=in_specs,
        out_specs=pl.BlockSpec((TM_FFN, D_MODEL), lambda i: (i, 0)),
        out_shape=jax.ShapeDtypeStruct((M_TOKENS, D_MODEL), F32),
        scratch_shapes=[pltpu.VMEM((TM_FFN, D_MODEL), F32), pltpu.VMEM((TM_FFN, D_MODEL), BF)],
        compiler_params=pltpu.CompilerParams(dimension_semantics=("parallel",),
                                             vmem_limit_bytes=VMEM_LIMIT),
        name="oproj_ffn")(x, oa, ob, wo, g.reshape(1, D_MODEL), win, wout, gf.reshape(1, D_MODEL))


def _fprep_kernel(fl_ref, b_ref, o_ref):
    x = fl_ref[0] + b_ref[...]
    lf = jnp.minimum(x, 0.0) - jnp.log1p(jnp.exp(-jnp.abs(x)))
    lft = lf.T
    r = lax.broadcasted_iota(jnp.int32, (TK, TK), 0)
    c = lax.broadcasted_iota(jnp.int32, (TK, TK), 1)
    u = jnp.where(r <= c, 1.0, 0.0).astype(BF)
    carry = jnp.zeros((N_HEADS, 1), F32)
    for cb in range(SEQ // TK):
        blk = lft[0:N_HEADS, cb * TK:(cb + 1) * TK]
        b0 = blk.astype(BF)
        r1 = blk - b0.astype(F32)
        b1 = r1.astype(BF)
        b2 = (r1 - b1.astype(F32)).astype(BF)
        cs = _dot(b0, u) + _dot(b1, u) + _dot(b2, u) + carry
        o_ref[0, :, cb * TK:(cb + 1) * TK] = cs * LOG2E
        carry = cs[:, TK - 1:TK]


def _fprep_call(flog, bias):
    return pl.pallas_call(
        _fprep_kernel, grid=(BATCH,),
        in_specs=[pl.BlockSpec((1, SEQ, LANES), lambda b: (b, 0, 0)),
                  pl.BlockSpec((1, LANES), lambda b: (0, 0))],
        out_specs=pl.BlockSpec((1, N_HEADS, SEQ), lambda b: (b, 0, 0)),
        out_shape=jax.ShapeDtypeStruct((BATCH, N_HEADS, SEQ), F32),
        compiler_params=pltpu.CompilerParams(dimension_semantics=("parallel",),
                                             vmem_limit_bytes=VMEM_LIMIT),
        name="forget_cumsum")(flog, bias)


def _tri_mask(rows, strict, row0=0):
    row = lax.broadcasted_iota(jnp.int32, (rows, TK), 0) + row0
    col = lax.broadcasted_iota(jnp.int32, (rows, TK), 1)
    return (col < row) if strict else (col <= row)


def _head_split(q):
    lane = lax.broadcasted_iota(jnp.int32, q.shape, 1)
    zero = jnp.zeros_like(q)
    return jnp.where(lane < HEAD_DIM, q, zero), jnp.where(lane >= HEAD_DIM, q, zero)


def _fox_kernel(q_ref, k_ref, v_ref, f_ref, o_ref, va_ref, vb_ref):
    pair = pl.program_id(1)
    lane_kv = lax.broadcasted_iota(jnp.int32, (SEQ, LANES), 1)
    v = v_ref[0]
    one = jnp.ones_like(v)
    va_ref[...] = jnp.where(lane_kv < HEAD_DIM, v, one)
    vb_ref[...] = jnp.where(lane_kv >= HEAD_DIM, v, one)
    vxs = (va_ref, vb_ref)
    diag_ok = _tri_mask(RC, False)
    is_a = lax.broadcasted_iota(jnp.int32, (RC, LANES), 1) < HEAD_DIM

    def qk(c):
        qh = _head_split(q_ref[0, c * RC:(c + 1) * RC, :])
        k = k_ref[0, 0:(c + 1) * RC, :]
        return [_dot_nt(qh[hh], k) for hh in (0, 1)]

    def softmax(c, zs):
        kend = (c + 1) * RC
        ps = []
        for hh in (0, 1):
            s = zs[hh] - f_ref[0, pl.ds(2 * pair + hh, 1), 0:kend]
            tail = jnp.where(diag_ok, s[:, kend - RC:], NEG_INF)
            s = tail if c == 0 else jnp.concatenate([s[:, :kend - RC], tail], axis=1)
            ps.append(jnp.exp2(s - jnp.max(s, axis=-1, keepdims=True)).astype(BF))
        return ps

    def pv(c, ps):
        outs = []
        for hh in (0, 1):
            o = _dot(ps[hh], vxs[hh][0:(c + 1) * RC, :])
            outs.append(o / pltpu.roll(o, HEAD_DIM, 1))
        o_ref[0, c * RC:(c + 1) * RC, :] = jnp.where(is_a, outs[0], outs[1]).astype(BF)

    n_chunks = SEQ // RC
    zs = {c: qk(c) for c in range(min(FOX_AHEAD, n_chunks))}
    for c in range(n_chunks):
        if c + FOX_AHEAD < n_chunks:
            zs[c + FOX_AHEAD] = qk(c + FOX_AHEAD)
        pv(c, softmax(c, zs.pop(c)))


def _fox_call(qkv3, f_row):
    n_pairs = N_HEADS // 2
    return pl.pallas_call(
        _fox_kernel, grid=(BATCH, n_pairs),
        in_specs=[pl.BlockSpec((1, SEQ, LANES), lambda b, p: (b, 0, p)),
                  pl.BlockSpec((1, SEQ, LANES), lambda b, p: (b, 0, n_pairs + p)),
                  pl.BlockSpec((1, SEQ, LANES), lambda b, p: (b, 0, 2 * n_pairs + p)),
                  pl.BlockSpec((1, N_HEADS, SEQ), lambda b, p: (b, 0, 0))],
        out_specs=pl.BlockSpec((1, SEQ, LANES), lambda b, p: (b, 0, p)),
        out_shape=jax.ShapeDtypeStruct((BATCH, SEQ, D_ATTN), BF),
        scratch_shapes=[pltpu.VMEM((SEQ, LANES), BF), pltpu.VMEM((SEQ, LANES), BF)],
        compiler_params=pltpu.CompilerParams(
            dimension_semantics=("parallel", "arbitrary"),
            vmem_limit_bytes=VMEM_LIMIT),
        name="fox_attn")(qkv3, qkv3, qkv3, f_row)


def _sb_kernel(q_ref, k_ref, v_ref, o_ref, u_ref):
    r = lax.broadcasted_iota(jnp.int32, (TK, TK), 0)
    c = lax.broadcasted_iota(jnp.int32, (TK, TK), 1)
    u_ref[...] = jnp.where(r > c, 1.0, 0.0)
    strict = _tri_mask(RC, True)
    lane = lax.broadcasted_iota(jnp.int32, (RC, LANES), 1)
    heads = (0, 1)

    def qk(c):
        qh = _head_split(q_ref[0, c * RC:(c + 1) * RC, :])
        z = _dot_nt(jnp.concatenate(qh, axis=0), k_ref[0, 0:(c + 1) * RC, :])
        return z[:RC], z[RC:]

    def gates(c, zs):
        nb = c + 1
        parts = []
        for z in zs:
            nz = -z
            lg = jnp.minimum(nz, 0.0) - jnp.log2(1.0 + jnp.exp2(jnp.minimum(z, nz)))
            arg = z + lg
            blocks = [lg[:, j * TK:(j + 1) * TK] for j in range(nb)]
            blocks[-1] = jnp.where(strict, blocks[-1], 0.0)
            carries, run = [None] * nb, None
            for j in reversed(range(nb)):
                carries[j] = run
                rs = jnp.sum(blocks[j], axis=-1, keepdims=True)
                run = rs if run is None else run + rs
            parts.append((arg, blocks, carries))
        suf = _dot(jnp.concatenate(parts[0][1] + parts[1][1], axis=0), u_ref[...])
        half = nb * RC
        return [(parts[hh][0], suf[hh * half:(hh + 1) * half], parts[hh][2]) for hh in heads]

    def values(c, gs):
        nb = c + 1
        weights = []
        for arg, suf, carries in gs:
            ws = []
            for j in range(nb):
                e = suf[j * RC:(j + 1) * RC]
                if carries[j] is not None:
                    e = e + carries[j]
                ws.append(jnp.exp2(arg[:, j * TK:(j + 1) * TK] + e))
            ws[-1] = jnp.where(strict, ws[-1], 0.0)
            weights.append((ws[0] if nb == 1 else jnp.concatenate(ws, axis=1)).astype(BF))
        o = _dot(jnp.concatenate(weights, axis=0), v_ref[0, 0:nb * TK, :])
        o_ref[0, c * RC:(c + 1) * RC, :] = jnp.where(lane < HEAD_DIM, o[:RC], o[RC:]).astype(BF)

    n_chunks = SEQ // RC
    zs = {0: qk(0)}
    gs = {0: gates(0, zs.pop(0))}
    if n_chunks > 1:
        zs[1] = qk(1)
    for c in range(n_chunks):
        if c + 2 < n_chunks:
            zs[c + 2] = qk(c + 2)
        if c + 1 < n_chunks:
            gs[c + 1] = gates(c + 1, zs.pop(c + 1))
        values(c, gs.pop(c))


def _sb_call(qkv3):
    n_pairs = N_HEADS // 2
    n_sb = n_pairs // 2
    return pl.pallas_call(
        _sb_kernel, grid=(BATCH, n_sb),
        in_specs=[pl.BlockSpec((1, SEQ, LANES), lambda b, p: (b, 0, p)),
                  pl.BlockSpec((1, SEQ, LANES), lambda b, p: (b, 0, n_pairs + p)),
                  pl.BlockSpec((1, SEQ, LANES), lambda b, p: (b, 0, 2 * n_pairs + p))],
        out_specs=pl.BlockSpec((1, SEQ, LANES), lambda b, p: (b, 0, p)),
        out_shape=jax.ShapeDtypeStruct((BATCH, SEQ, D_ATTN // 2), BF),
        scratch_shapes=[pltpu.VMEM((TK, TK), F32)],
        compiler_params=pltpu.CompilerParams(
            dimension_semantics=("parallel", "arbitrary"),
            vmem_limit_bytes=VMEM_LIMIT),
        name="stickbreak_attn")(qkv3, qkv3, qkv3)


def _dil_kernel(q_ref, k_ref, v_ref, o_ref,
                nat_ref, qp_ref, kp_ref, vp_ref, xo_ref, xl_ref, on_ref):
    lane = lax.broadcasted_iota(jnp.int32, (DB, LANES), 1)
    is_a = lane < HEAD_DIM

    for t in range(N_DB):
        rows = slice(t * DB, (t + 1) * DB)
        nat_ref[0, rows, :] = q_ref[0, rows, :].astype(F32)
        nat_ref[1, rows, :] = k_ref[0, rows, :].astype(F32)
        nat_ref[2, rows, :] = v_ref[0, rows, :].astype(F32)

    for pi, (window, d) in enumerate(DIL_PATTERNS):
        per_res = SEQ // d // DB
        for t in range(N_DB):
            r, c = divmod(t, per_res)
            src = pl.ds(r + d * DB * c, DB, stride=d) if d > 1 else pl.ds(t * DB, DB)
            dst = slice(t * DB, (t + 1) * DB)
            qp_ref[pi, dst, :] = nat_ref[0, src, :].astype(BF)
            kp_ref[pi, dst, :] = nat_ref[1, src, :].astype(BF)
            vp_ref[pi, dst, :] = nat_ref[2, src, :].astype(BF)

    row = lax.broadcasted_iota(jnp.int32, (DB, DB), 0)
    col = lax.broadcasted_iota(jnp.int32, (DB, DB), 1)
    win_ok = jnp.concatenate([col >= row, col <= row], axis=1)
    cur_ok = col <= row
    ones = jnp.ones((2 * DB, LANES), BF)

    def window(t, pi):
        per_res = SEQ // DIL_PATTERNS[pi][1] // DB
        has_prev = t % per_res != 0
        return slice((t - 1 if has_prev else t) * DB, (t + 1) * DB)

    for g0 in range(0, N_DB, DIL_GROUP):
        items = [(t, pi) for t in range(g0, g0 + DIL_GROUP) for pi in range(len(DIL_PATTERNS))]
        chains = [(t, pi, hh) for t, pi in items for hh in (0, 1)]
        scores = {}
        for t, pi in items:
            qb = qp_ref[pi, t * DB:(t + 1) * DB, :]
            kw = kp_ref[pi, window(t, pi), :]
            qa, qbb = _head_split(qb)
            scores[(t, pi, 0)] = _dot_nt(qa, kw)
            scores[(t, pi, 1)] = _dot_nt(qbb, kw)
        probs, maxes = {}, {}
        for key in chains:
            s = scores.pop(key)
            s = jnp.where(win_ok if s.shape[1] == 2 * DB else cur_ok, s, NEG_INF)
            m = jnp.max(s, axis=-1, keepdims=True)
            probs[key] = jnp.exp2(s - m).astype(BF)
            maxes[key] = m
        nums, dens = {}, {}
        for t, pi, hh in chains:
            p = probs.pop((t, pi, hh))
            nums[(t, pi, hh)] = _dot(p, vp_ref[pi, window(t, pi), :])
            dens[(t, pi, hh)] = _dot(p, ones[:p.shape[1]])
        for t, pi in items:
            ka, kb = (t, pi, 0), (t, pi, 1)
            den = jnp.where(is_a, dens.pop(ka), dens.pop(kb))
            rows = slice(t * DB, (t + 1) * DB)
            xo_ref[pi, rows, :] = jnp.where(is_a, nums.pop(ka), nums.pop(kb)) / den
            xl_ref[pi, rows, :] = jnp.where(is_a, maxes.pop(ka), maxes.pop(kb)) + jnp.log2(den)

    d_max = DIL_PATTERNS[-1][1]
    for r in range(d_max):
        xs, ls = [], []
        for pi, (window_len, d) in enumerate(DIL_PATTERNS):
            step = d_max // d
            start = (r % d) * (SEQ // d) + r // d
            rows = pl.ds(start, DB, stride=step) if step > 1 else pl.ds(start, DB)
            xs.append(xo_ref[pi, rows, :])
            ls.append(xl_ref[pi, rows, :])
        mx = functools.reduce(jnp.maximum, ls)
        es = [jnp.exp2(l - mx) for l in ls]
        inv = 1.0 / functools.reduce(lambda a, b: a + b, es)
        on_ref[pl.ds(r, DB, stride=d_max), :] = functools.reduce(
            lambda a, b: a + b, [(e * inv) * x for e, x in zip(es, xs)])

    o_ref[0] = on_ref[...].astype(BF)


def _dil_call(qkv3):
    n_pairs = N_HEADS // 2
    n_sb = n_pairs // 2
    n_pat = len(DIL_PATTERNS)
    return pl.pallas_call(
        _dil_kernel, grid=(BATCH, n_pairs - n_sb),
        in_specs=[pl.BlockSpec((1, SEQ, LANES), lambda b, p: (b, 0, n_sb + p)),
                  pl.BlockSpec((1, SEQ, LANES), lambda b, p: (b, 0, n_pairs + n_sb + p)),
                  pl.BlockSpec((1, SEQ, LANES), lambda b, p: (b, 0, 2 * n_pairs + n_sb + p))],
        out_specs=pl.BlockSpec((1, SEQ, LANES), lambda b, p: (b, 0, p)),
        out_shape=jax.ShapeDtypeStruct((BATCH, SEQ, D_ATTN // 2), BF),
        scratch_shapes=[pltpu.VMEM((3, SEQ, LANES), F32),
                        pltpu.VMEM((n_pat, SEQ, LANES), BF),
                        pltpu.VMEM((n_pat, SEQ, LANES), BF),
                        pltpu.VMEM((n_pat, SEQ, LANES), BF),
                        pltpu.VMEM((n_pat, SEQ, LANES), F32),
                        pltpu.VMEM((n_pat, SEQ, LANES), F32),
                        pltpu.VMEM((SEQ, LANES), F32)],
        compiler_params=pltpu.CompilerParams(dimension_semantics=("parallel", "arbitrary"),
                                             vmem_limit_bytes=VMEM_LIMIT),
        name="dilated_attn")(qkv3, qkv3, qkv3)


def _rotary_tables():
    half = ROT_DIM // 2
    pos = jnp.arange(SEQ, dtype=F32)
    inv_freq = ROPE_THETA ** (-jnp.arange(half, dtype=F32) * 2.0 / ROT_DIM)
    ang = pos[:, None] * inv_freq[None, :]
    cos, sin = jnp.cos(ang), jnp.sin(ang)
    zeros = jnp.zeros((SEQ, HEAD_DIM - ROT_DIM), F32)
    z8 = jnp.zeros((SEQ, half), F32)
    c_head = jnp.concatenate([cos, cos, zeros + 1.0], axis=1)
    s1_head = jnp.concatenate([-sin, z8, zeros], axis=1)
    s2_head = jnp.concatenate([z8, sin, zeros], axis=1)
    two = lambda t: jnp.concatenate([t, t], axis=1)
    return two(c_head), two(s1_head), two(s2_head)


def kernel(x, norm_mix, w_qkv_even, w_o_even, w_qkvf_odd, b_forget, w_o_odd, norm_ffn,
           w_ffn_in, w_ffn_out, norm_final):
    h = x.reshape(M_TOKENS, D_MODEL)
    cos_t, sin1_t, sin2_t = _rotary_tables()
    n_qkv = 3 * D_ATTN
    w_qkv_even = w_qkv_even.astype(BF)
    for layer in range(DEPTH):
        i = layer // 2
        casts = [(w_o_even if layer % 2 == 0 else w_o_odd, i), (w_ffn_in, layer), (w_ffn_out, layer)]
        if layer % 2 == 0:
            qkv, wo, win, wout = _qkv_call(h, norm_mix[layer], w_qkv_even, layer=i,
                                           rot_tables=(cos_t, sin1_t, sin2_t), casts=casts)
            qkv3 = qkv.reshape(BATCH, SEQ, n_qkv)
            oa = _sb_call(qkv3).reshape(M_TOKENS, D_ATTN // 2)
            ob = _dil_call(qkv3).reshape(M_TOKENS, D_ATTN // 2)
            ob_col = 0
        else:
            w = w_qkvf_odd[i]
            wf = jnp.pad(w[:, n_qkv:], ((0, 0), (0, LANES - N_HEADS))).astype(BF)
            qkv, flog, wo, win, wout = _qkv_call(h, norm_mix[layer], w[:, :n_qkv].astype(BF), wf,
                                                 casts=casts)
            bias = jnp.pad(b_forget[i], (0, LANES - N_HEADS)).reshape(1, LANES)
            f_row = _fprep_call(flog.reshape(BATCH, SEQ, LANES), bias)
            o = _fox_call(qkv.reshape(BATCH, SEQ, n_qkv), f_row).reshape(M_TOKENS, D_ATTN)
            oa, ob, ob_col = o, o, 1
        h = _ffn_call(h, oa, ob, ob_col, wo, norm_ffn[layer], win, wout, norm_final,
                      final_norm=(layer == DEPTH - 1))
    return h.reshape(BATCH, SEQ, D_MODEL)
```

```python
import functools
import math

import jax
import jax.numpy as jnp
from jax import lax
from jax.experimental import pallas as pl
from jax.experimental.pallas import tpu as pltpu

D_MODEL = 1024
BATCH = 8
SEQ = 2048
DEPTH = 4
HEAD_DIM = 64
N_HEADS = 16
D_ATTN = N_HEADS * HEAD_DIM
D_FF = 2816
ROPE_THETA = 500000.0
ROT_DIM = HEAD_DIM // 4
DIL_PATTERNS = ((128, 1), (512, 4), (2048, 16))
RMS_EPS = 1e-5
LOG2E = math.log2(math.e)
Q_SCALE = HEAD_DIM ** -0.5 * LOG2E

LANES = 128
BF16_ROWS = 16
M_TOKENS = BATCH * SEQ
TM = 512
TM_FFN = 1024
TN = 1024
TF = 512
TQ = 1024
TK = 256
RC = 256
DB = 128
N_DB = SEQ // DB
DIL_GROUP = 8
FOX_AHEAD = 1
VMEM_LIMIT = 56 * 1024 * 1024

BF = jnp.bfloat16
F32 = jnp.float32
NEG_INF = float("-inf")


def _rms(x, g):
    ms = jnp.mean(x * x, axis=-1, keepdims=True)
    return x * lax.rsqrt(ms + RMS_EPS) * g


def _dot(a, b):
    return jnp.dot(a, b, preferred_element_type=F32)


def _dot_nt(a, b):
    return lax.dot_general(a, b, (((1,), (1,)), ((), ())), preferred_element_type=F32)


def _set_rows(full, r0, part):
    return part if r0 == 0 else jnp.concatenate([full[:r0], part], axis=0)


def _rotate(x, c, s1, s2):
    half = ROT_DIM // 2
    return x * c + pltpu.roll(x, LANES - half, 1) * s1 + pltpu.roll(x, half, 1) * s2


def _qkv_kernel(x_ref, g_ref, w_ref, *rest, has_forget, rotary, n_cast):
    n_extra = int(has_forget) + 3 * int(rotary)
    ins, outs = rest[:n_extra + n_cast], rest[n_extra + n_cast:]
    hn = _rms(x_ref[...], g_ref[...]).astype(BF)
    o_ref = outs[0]
    n_tiles = D_ATTN // LANES
    for j in range(3 * D_ATTN // TN):
        y = _dot(hn, w_ref[:, j * TN:(j + 1) * TN])
        if (j + 1) * TN <= D_ATTN:
            y = y * Q_SCALE
        if rotary and (j + 1) * TN <= 2 * D_ATTN:
            c, s1, s2 = (r[...] for r in ins[int(has_forget):n_extra])
            tiles = [y[:, t * LANES:(t + 1) * LANES] for t in range(n_tiles)]
            tiles = [x if t < n_tiles // 2 else _rotate(x, c, s1, s2) for t, x in enumerate(tiles)]
            y = jnp.concatenate(tiles, axis=1)
        o_ref[:, j * TN:(j + 1) * TN] = y.astype(BF)
    if has_forget:
        outs[1][...] = _dot(hn, ins[0][...])
    for src, dst in zip(ins[n_extra:], outs[1 + int(has_forget):]):
        dst[...] = src[...].astype(BF)


def _const_spec(shape, layer=None):
    if layer is None:
        return pl.BlockSpec(shape, lambda *_: (0,) * len(shape), pipeline_mode=pl.Buffered(1))
    return pl.BlockSpec((None,) + shape, lambda *_: (layer,) + (0,) * len(shape),
                        pipeline_mode=pl.Buffered(1))


def _qkv_call(x, g, w, wf=None, layer=None, rot_tables=None, casts=()):
    n = 3 * D_ATTN
    steps = M_TOKENS // TM
    in_specs = [pl.BlockSpec((TM, D_MODEL), lambda i: (i, 0)),
                _const_spec((1, D_MODEL)),
                _const_spec((D_MODEL, n), layer)]
    out_specs = [pl.BlockSpec((TM, n), lambda i: (i, 0))]
    out_shape = [jax.ShapeDtypeStruct((M_TOKENS, n), BF)]
    args = [x, g.reshape(1, D_MODEL), w]
    if wf is not None:
        in_specs.append(_const_spec((D_MODEL, LANES)))
        out_specs.append(pl.BlockSpec((TM, LANES), lambda i: (i, 0)))
        out_shape.append(jax.ShapeDtypeStruct((M_TOKENS, LANES), F32))
        args.append(wf)
    if rot_tables is not None:
        for tab in rot_tables:
            in_specs.append(pl.BlockSpec((TM, LANES), lambda i: (i % (SEQ // TM), 0)))
            args.append(tab)
    for stack, idx in casts:
        _, rows, cols = stack.shape
        hold = 1
        while (rows * hold // steps) % BF16_ROWS:
            hold *= 2
        blk = rows * hold // steps
        in_specs.append(pl.BlockSpec((None, blk, cols), lambda i, idx=idx, hold=hold: (idx, i // hold, 0)))
        out_specs.append(pl.BlockSpec((blk, cols), lambda i, hold=hold: (i // hold, 0)))
        out_shape.append(jax.ShapeDtypeStruct((rows, cols), BF))
        args.append(stack)
    outs = pl.pallas_call(
        functools.partial(_qkv_kernel, has_forget=wf is not None, rotary=rot_tables is not None,
                          n_cast=len(casts)),
        grid=(steps,), in_specs=in_specs, out_specs=out_specs, out_shape=out_shape,
        compiler_params=pltpu.CompilerParams(dimension_semantics=("arbitrary",),
                                             vmem_limit_bytes=VMEM_LIMIT),
        name="norm_qkv" if wf is None else "norm_qkvf")(*args)
    return list(outs)


def _ffn_kernel(x_ref, oa_ref, ob_ref, wo_ref, g_ref, win_ref, wout_ref, gf_ref, out_ref,
                acc_ref, hn_ref, *, final_norm):
    half = D_ATTN // 2
    xn = x_ref[...] + _dot(oa_ref[...], wo_ref[0:half, :]) + _dot(ob_ref[...], wo_ref[half:, :])
    acc_ref[...] = xn
    hn_ref[...] = _rms(xn, g_ref[...]).astype(BF)

    def tile(c0, width):
        hn = hn_ref[...]
        c1 = D_FF + c0
        if not isinstance(c0, int):
            c1 = pl.multiple_of(c1, math.gcd(D_FF, TF))
        g = _dot(hn, win_ref[:, pl.ds(c0, width)])
        u = _dot(hn, win_ref[:, pl.ds(c1, width)])
        a = (g * (1.0 / (1.0 + jnp.exp(-g))) * u).astype(BF)
        acc_ref[...] += _dot(a, wout_ref[pl.ds(c0, width), :])

    def body(f, c):
        tile(pl.multiple_of(f * TF, TF), TF)
        return c

    n_tiles = D_FF // TF
    lax.fori_loop(0, n_tiles, body, 0)
    if D_FF > n_tiles * TF:
        tile(n_tiles * TF, D_FF - n_tiles * TF)
    y = acc_ref[...]
    if final_norm:
        y = _rms(y, gf_ref[...])
    out_ref[...] = y


def _ffn_call(x, oa, ob, ob_col, wo, g, win, wout, gf, final_norm):
    half = D_ATTN // 2
    in_specs = [pl.BlockSpec((TM_FFN, D_MODEL), lambda i: (i, 0)),
                pl.BlockSpec((TM_FFN, half), lambda i: (i, 0)),
                pl.BlockSpec((TM_FFN, half), lambda i: (i, ob_col)),
                _const_spec((D_ATTN, D_MODEL)),
                _const_spec((1, D_MODEL)),
                _const_spec((D_MODEL, 2 * D_FF)),
                _const_spec((D_FF, D_MODEL)),
                _const_spec((1, D_MODEL))]
    return pl.pallas_call(
        functools.partial(_ffn_kernel, final_norm=final_norm),
        grid=(M_TOKENS // TM_FFN,), in_specs=in_specs,
        out_specs=pl.BlockSpec((TM_FFN, D_MODEL), lambda i: (i, 0)),
        out_shape=jax.ShapeDtypeStruct((M_TOKENS, D_MODEL), F32),
        scratch_shapes=[pltpu.VMEM((TM_FFN, D_MODEL), F32), pltpu.VMEM((TM_FFN, D_MODEL), BF)],
        compiler_params=pltpu.CompilerParams(dimension_semantics=("parallel",),
                                             vmem_limit_bytes=VMEM_LIMIT),
        name="oproj_ffn")(x, oa, ob, wo, g.reshape(1, D_MODEL), win, wout, gf.reshape(1, D_MODEL))


def _fprep_kernel(fl_ref, b_ref, o_ref):
    x = fl_ref[0] + b_ref[...]
    lf = jnp.minimum(x, 0.0) - jnp.log1p(jnp.exp(-jnp.abs(x)))
    lft = lf.T
    r = lax.broadcasted_iota(jnp.int32, (TK, TK), 0)
    c = lax.broadcasted_iota(jnp.int32, (TK, TK), 1)
    u = jnp.where(r <= c, 1.0, 0.0).astype(BF)
    carry = jnp.zeros((N_HEADS, 1), F32)
    for cb in range(SEQ // TK):
        blk = lft[0:N_HEADS, cb * TK:(cb + 1) * TK]
        b0 = blk.astype(BF)
        r1 = blk - b0.astype(F32)
        b1 = r1.astype(BF)
        b2 = (r1 - b1.astype(F32)).astype(BF)
        cs = _dot(b0, u) + _dot(b1, u) + _dot(b2, u) + carry
        o_ref[0, :, cb * TK:(cb + 1) * TK] = cs * LOG2E
        carry = cs[:, TK - 1:TK]


def _fprep_call(flog, bias):
    return pl.pallas_call(
        _fprep_kernel, grid=(BATCH,),
        in_specs=[pl.BlockSpec((1, SEQ, LANES), lambda b: (b, 0, 0)),
                  pl.BlockSpec((1, LANES), lambda b: (0, 0))],
        out_specs=pl.BlockSpec((1, N_HEADS, SEQ), lambda b: (b, 0, 0)),
        out_shape=jax.ShapeDtypeStruct((BATCH, N_HEADS, SEQ), F32),
        compiler_params=pltpu.CompilerParams(dimension_semantics=("parallel",),
                                             vmem_limit_bytes=VMEM_LIMIT),
        name="forget_cumsum")(flog, bias)


def _tri_mask(rows, strict, row0=0):
    row = lax.broadcasted_iota(jnp.int32, (rows, TK), 0) + row0
    col = lax.broadcasted_iota(jnp.int32, (rows, TK), 1)
    return (col < row) if strict else (col <= row)


def _head_split(q):
    lane = lax.broadcasted_iota(jnp.int32, q.shape, 1)
    zero = jnp.zeros_like(q)
    return jnp.where(lane < HEAD_DIM, q, zero), jnp.where(lane >= HEAD_DIM, q, zero)


def _fox_kernel(q_ref, k_ref, v_ref, f_ref, o_ref, va_ref, vb_ref):
    pair = pl.program_id(1)
    lane_kv = lax.broadcasted_iota(jnp.int32, (SEQ, LANES), 1)
    v = v_ref[0]
    one = jnp.ones_like(v)
    va_ref[...] = jnp.where(lane_kv < HEAD_DIM, v, one)
    vb_ref[...] = jnp.where(lane_kv >= HEAD_DIM, v, one)
    vxs = (va_ref, vb_ref)
    diag_ok = _tri_mask(RC, False)
    is_a = lax.broadcasted_iota(jnp.int32, (RC, LANES), 1) < HEAD_DIM

    def qk(c):
        qh = _head_split(q_ref[0, c * RC:(c + 1) * RC, :])
        k = k_ref[0, 0:(c + 1) * RC, :]
        return [_dot_nt(qh[hh], k) for hh in (0, 1)]

    def softmax(c, zs):
        kend = (c + 1) * RC
        ps = []
        for hh in (0, 1):
            s = zs[hh] - f_ref[0, pl.ds(2 * pair + hh, 1), 0:kend]
            tail = jnp.where(diag_ok, s[:, kend - RC:], NEG_INF)
            s = tail if c == 0 else jnp.concatenate([s[:, :kend - RC], tail], axis=1)
            ps.append(jnp.exp2(s - jnp.max(s, axis=-1, keepdims=True)).astype(BF))
        return ps

    def pv(c, ps):
        outs = []
        for hh in (0, 1):
            o = _dot(ps[hh], vxs[hh][0:(c + 1) * RC, :])
            outs.append(o / pltpu.roll(o, HEAD_DIM, 1))
        o_ref[0, c * RC:(c + 1) * RC, :] = jnp.where(is_a, outs[0], outs[1]).astype(BF)

    n_chunks = SEQ // RC
    zs = {c: qk(c) for c in range(min(FOX_AHEAD, n_chunks))}
    for c in range(n_chunks):
        if c + FOX_AHEAD < n_chunks:
            zs[c + FOX_AHEAD] = qk(c + FOX_AHEAD)
        pv(c, softmax(c, zs.pop(c)))


def _fox_call(qkv3, f_row):
    n_pairs = N_HEADS // 2
    return pl.pallas_call(
        _fox_kernel, grid=(BATCH, n_pairs),
        in_specs=[pl.BlockSpec((1, SEQ, LANES), lambda b, p: (b, 0, p)),
                  pl.BlockSpec((1, SEQ, LANES), lambda b, p: (b, 0, n_pairs + p)),
                  pl.BlockSpec((1, SEQ, LANES), lambda b, p: (b, 0, 2 * n_pairs + p)),
                  pl.BlockSpec((1, N_HEADS, SEQ), lambda b, p: (b, 0, 0))],
        out_specs=pl.BlockSpec((1, SEQ, LANES), lambda b, p: (b, 0, p)),
        out_shape=jax.ShapeDtypeStruct((BATCH, SEQ, D_ATTN), BF),
        scratch_shapes=[pltpu.VMEM((SEQ, LANES), BF), pltpu.VMEM((SEQ, LANES), BF)],
        compiler_params=pltpu.CompilerParams(
            dimension_semantics=("parallel", "arbitrary"),
            vmem_limit_bytes=VMEM_LIMIT),
        name="fox_attn")(qkv3, qkv3, qkv3, f_row)


def _sb_kernel(q_ref, k_ref, v_ref, o_ref, u_ref):
    r = lax.broadcasted_iota(jnp.int32, (TK, TK), 0)
    c = lax.broadcasted_iota(jnp.int32, (TK, TK), 1)
    u_ref[...] = jnp.where(r > c, 1.0, 0.0)
    strict = _tri_mask(RC, True)
    lane = lax.broadcasted_iota(jnp.int32, (RC, LANES), 1)
    heads = (0, 1)

    def qk(c):
        qh = _head_split(q_ref[0, c * RC:(c + 1) * RC, :])
        z = _dot_nt(jnp.concatenate(qh, axis=0), k_ref[0, 0:(c + 1) * RC, :])
        return z[:RC], z[RC:]

    def gates(c, zs):
        nb = c + 1
        parts = []
        for z in zs:
            nz = -z
            lg = jnp.minimum(nz, 0.0) - jnp.log2(1.0 + jnp.exp2(jnp.minimum(z, nz)))
            arg = z + lg
            blocks = [lg[:, j * TK:(j + 1) * TK] for j in range(nb)]
            blocks[-1] = jnp.where(strict, blocks[-1], 0.0)
            carries, run = [None] * nb, None
            for j in reversed(range(nb)):
                carries[j] = run
                rs = jnp.sum(blocks[j], axis=-1, keepdims=True)
                run = rs if run is None else run + rs
            parts.append((arg, blocks, carries))
        suf = _dot(jnp.concatenate(parts[0][1] + parts[1][1], axis=0), u_ref[...])
        half = nb * RC
        return [(parts[hh][0], suf[hh * half:(hh + 1) * half], parts[hh][2]) for hh in heads]

    def values(c, gs):
        nb = c + 1
        weights = []
        for arg, suf, carries in gs:
            ws = []
            for j in range(nb):
                e = suf[j * RC:(j + 1) * RC]
                if carries[j] is not None:
                    e = e + carries[j]
                ws.append(jnp.exp2(arg[:, j * TK:(j + 1) * TK] + e))
            ws[-1] = jnp.where(strict, ws[-1], 0.0)
            weights.append((ws[0] if nb == 1 else jnp.concatenate(ws, axis=1)).astype(BF))
        o = _dot(jnp.concatenate(weights, axis=0), v_ref[0, 0:nb * TK, :])
        o_ref[0, c * RC:(c + 1) * RC, :] = jnp.where(lane < HEAD_DIM, o[:RC], o[RC:]).astype(BF)

    n_chunks = SEQ // RC
    zs = {0: qk(0)}
    gs = {0: gates(0, zs.pop(0))}
    if n_chunks > 1:
        zs[1] = qk(1)
    for c in range(n_chunks):
        if c + 2 < n_chunks:
            zs[c + 2] = qk(c + 2)
        if c + 1 < n_chunks:
            gs[c + 1] = gates(c + 1, zs.pop(c + 1))
        values(c, gs.pop(c))


def _sb_call(qkv3):
    n_pairs = N_HEADS // 2
    n_sb = n_pairs // 2
    return pl.pallas_call(
        _sb_kernel, grid=(BATCH, n_sb),
        in_specs=[pl.BlockSpec((1, SEQ, LANES), lambda b, p: (b, 0, p)),
                  pl.BlockSpec((1, SEQ, LANES), lambda b, p: (b, 0, n_pairs + p)),
                  pl.BlockSpec((1, SEQ, LANES), lambda b, p: (b, 0, 2 * n_pairs + p))],
        out_specs=pl.BlockSpec((1, SEQ, LANES), lambda b, p: (b, 0, p)),
        out_shape=jax.ShapeDtypeStruct((BATCH, SEQ, D_ATTN // 2), BF),
        scratch_shapes=[pltpu.VMEM((TK, TK), F32)],
        compiler_params=pltpu.CompilerParams(
            dimension_semantics=("parallel", "arbitrary"),
            vmem_limit_bytes=VMEM_LIMIT),
        name="stickbreak_attn")(qkv3, qkv3, qkv3)


def _dil_kernel(q_ref, k_ref, v_ref, o_ref,
                nat_ref, qp_ref, kp_ref, vp_ref, xo_ref, xl_ref, on_ref):
    lane = lax.broadcasted_iota(jnp.int32, (DB, LANES), 1)
    is_a = lane < HEAD_DIM

    for t in range(N_DB):
        rows = slice(t * DB, (t + 1) * DB)
        nat_ref[0, rows, :] = q_ref[0, rows, :].astype(F32)
        nat_ref[1, rows, :] = k_ref[0, rows, :].astype(F32)
        nat_ref[2, rows, :] = v_ref[0, rows, :].astype(F32)

    for pi, (window, d) in enumerate(DIL_PATTERNS):
        per_res = SEQ // d // DB
        for t in range(N_DB):
            r, c = divmod(t, per_res)
            src = pl.ds(r + d * DB * c, DB, stride=d) if d > 1 else pl.ds(t * DB, DB)
            dst = slice(t * DB, (t + 1) * DB)
            qp_ref[pi, dst, :] = nat_ref[0, src, :].astype(BF)
            kp_ref[pi, dst, :] = nat_ref[1, src, :].astype(BF)
            vp_ref[pi, dst, :] = nat_ref[2, src, :].astype(BF)

    row = lax.broadcasted_iota(jnp.int32, (DB, DB), 0)
    col = lax.broadcasted_iota(jnp.int32, (DB, DB), 1)
    win_ok = jnp.concatenate([col >= row, col <= row], axis=1)
    cur_ok = col <= row
    ones = jnp.ones((2 * DB, LANES), BF)

    def window(t, pi):
        per_res = SEQ // DIL_PATTERNS[pi][1] // DB
        has_prev = t % per_res != 0
        return slice((t - 1 if has_prev else t) * DB, (t + 1) * DB)

    for g0 in range(0, N_DB, DIL_GROUP):
        items = [(t, pi) for t in range(g0, g0 + DIL_GROUP) for pi in range(len(DIL_PATTERNS))]
        chains = [(t, pi, hh) for t, pi in items for hh in (0, 1)]
        scores = {}
        for t, pi in items:
            qb = qp_ref[pi, t * DB:(t + 1) * DB, :]
            kw = kp_ref[pi, window(t, pi), :]
            qa, qbb = _head_split(qb)
            scores[(t, pi, 0)] = _dot_nt(qa, kw)
            scores[(t, pi, 1)] = _dot_nt(qbb, kw)
        probs, maxes = {}, {}
        for key in chains:
            s = scores.pop(key)
            s = jnp.where(win_ok if s.shape[1] == 2 * DB else cur_ok, s, NEG_INF)
            m = jnp.max(s, axis=-1, keepdims=True)
            probs[key] = jnp.exp2(s - m).astype(BF)
            maxes[key] = m
        nums, dens = {}, {}
        for t, pi, hh in chains:
            p = probs.pop((t, pi, hh))
            nums[(t, pi, hh)] = _dot(p, vp_ref[pi, window(t, pi), :])
            dens[(t, pi, hh)] = _dot(p, ones[:p.shape[1]])
        for t, pi in items:
            ka, kb = (t, pi, 0), (t, pi, 1)
            den = jnp.where(is_a, dens.pop(ka), dens.pop(kb))
            rows = slice(t * DB, (t + 1) * DB)
            xo_ref[pi, rows, :] = jnp.where(is_a, nums.pop(ka), nums.pop(kb)) / den
            xl_ref[pi, rows, :] = jnp.where(is_a, maxes.pop(ka), maxes.pop(kb)) + jnp.log2(den)

    d_max = DIL_PATTERNS[-1][1]
    for r in range(d_max):
        xs, ls = [], []
        for pi, (window_len, d) in enumerate(DIL_PATTERNS):
            step = d_max // d
            start = (r % d) * (SEQ // d) + r // d
            rows = pl.ds(start, DB, stride=step) if step > 1 else pl.ds(start, DB)
            xs.append(xo_ref[pi, rows, :])
            ls.append(xl_ref[pi, rows, :])
        mx = functools.reduce(jnp.maximum, ls)
        es = [jnp.exp2(l - mx) for l in ls]
        inv = 1.0 / functools.reduce(lambda a, b: a + b, es)
        on_ref[pl.ds(r, DB, stride=d_max), :] = functools.reduce(
            lambda a, b: a + b, [(e * inv) * x for e, x in zip(es, xs)])

    o_ref[0] = on_ref[...].astype(BF)


def _dil_call(qkv3):
    n_pairs = N_HEADS // 2
    n_sb = n_pairs // 2
    n_pat = len(DIL_PATTERNS)
    return pl.pallas_call(
        _dil_kernel, grid=(BATCH, n_pairs - n_sb),
        in_specs=[pl.BlockSpec((1, SEQ, LANES), lambda b, p: (b, 0, n_sb + p)),
                  pl.BlockSpec((1, SEQ, LANES), lambda b, p: (b, 0, n_pairs + n_sb + p)),
                  pl.BlockSpec((1, SEQ, LANES), lambda b, p: (b, 0, 2 * n_pairs + n_sb + p))],
        out_specs=pl.BlockSpec((1, SEQ, LANES), lambda b, p: (b, 0, p)),
        out_shape=jax.ShapeDtypeStruct((BATCH, SEQ, D_ATTN // 2), BF),
        scratch_shapes=[pltpu.VMEM((3, SEQ, LANES), F32),
                        pltpu.VMEM((n_pat, SEQ, LANES), BF),
                        pltpu.VMEM((n_pat, SEQ, LANES), BF),
                        pltpu.VMEM((n_pat, SEQ, LANES), BF),
                        pltpu.VMEM((n_pat, SEQ, LANES), F32),
                        pltpu.VMEM((n_pat, SEQ, LANES), F32),
                        pltpu.VMEM((SEQ, LANES), F32)],
        compiler_params=pltpu.CompilerParams(dimension_semantics=("parallel", "arbitrary"),
                                             vmem_limit_bytes=VMEM_LIMIT),
        name="dilated_attn")(qkv3, qkv3, qkv3)


def _rotary_tables():
    half = ROT_DIM // 2
    pos = jnp.arange(SEQ, dtype=F32)
    inv_freq = ROPE_THETA ** (-jnp.arange(half, dtype=F32) * 2.0 / ROT_DIM)
    ang = pos[:, None] * inv_freq[None, :]
    cos, sin = jnp.cos(ang), jnp.sin(ang)
    zeros = jnp.zeros((SEQ, HEAD_DIM - ROT_DIM), F32)
    z8 = jnp.zeros((SEQ, half), F32)
    c_head = jnp.concatenate([cos, cos, zeros + 1.0], axis=1)
    s1_head = jnp.concatenate([-sin, z8, zeros], axis=1)
    s2_head = jnp.concatenate([z8, sin, zeros], axis=1)
    two = lambda t: jnp.concatenate([t, t], axis=1)
    return two(c_head), two(s1_head), two(s2_head)


def kernel(x, norm_mix, w_qkv_even, w_o_even, w_qkvf_odd, b_forget, w_o_odd, norm_ffn,
           w_ffn_in, w_ffn_out, norm_final):
    h = x.reshape(M_TOKENS, D_MODEL)
    cos_t, sin1_t, sin2_t = _rotary_tables()
    n_qkv = 3 * D_ATTN
    w_qkv_even = w_qkv_even.astype(BF)
    for layer in range(DEPTH):
        i = layer // 2
        casts = [(w_o_even if layer % 2 == 0 else w_o_odd, i), (w_ffn_in, layer), (w_ffn_out, layer)]
        if layer % 2 == 0:
            qkv, wo, win, wout = _qkv_call(h, norm_mix[layer], w_qkv_even, layer=i,
                                           rot_tables=(cos_t, sin1_t, sin2_t), casts=casts)
            qkv3 = qkv.reshape(BATCH, SEQ, n_qkv)
            oa = _sb_call(qkv3).reshape(M_TOKENS, D_ATTN // 2)
            ob = _dil_call(qkv3).reshape(M_TOKENS, D_ATTN // 2)
            ob_col = 0
        else:
            w = w_qkvf_odd[i]
            wf = jnp.pad(w[:, n_qkv:], ((0, 0), (0, LANES - N_HEADS))).astype(BF)
            qkv, flog, wo, win, wout = _qkv_call(h, norm_mix[layer], w[:, :n_qkv].astype(BF), wf,
                                                 casts=casts)
            bias = jnp.pad(b_forget[i], (0, LANES - N_HEADS)).reshape(1, LANES)
            f_row = _fprep_call(flog.reshape(BATCH, SEQ, LANES), bias)
            o = _fox_call(qkv.reshape(BATCH, SEQ, n_qkv), f_row).reshape(M_TOKENS, D_ATTN)
            oa, ob, ob_col = o, o, 1
        h = _ffn_call(h, oa, ob, ob_col, wo, norm_ffn[layer], win, wout, norm_final,
                      final_norm=(layer == DEPTH - 1))
    return h.reshape(BATCH, SEQ, D_MODEL)
```

```python
import functools
import math

import jax
import jax.numpy as jnp
from jax import lax
from jax.experimental import pallas as pl
from jax.experimental.pallas import tpu as pltpu

D_MODEL = 1024
BATCH = 8
SEQ = 2048
DEPTH = 4
HEAD_DIM = 64
N_HEADS = 16
D_ATTN = N_HEADS * HEAD_DIM
D_FF = 2816
ROPE_THETA = 500000.0
ROT_DIM = HEAD_DIM // 4
DIL_PATTERNS = ((128, 1), (512, 4), (2048, 16))
RMS_EPS = 1e-5
LOG2E = math.log2(math.e)
Q_SCALE = HEAD_DIM ** -0.5 * LOG2E

LANES = 128
BF16_ROWS = 16
M_TOKENS = BATCH * SEQ
TM = 1024
TM_FFN = 1024
TN = 1024
TF = 512
TQ = 1024
TK = 256
RC = 256
DB = 128
N_DB = SEQ // DB
DIL_GROUP = 8
FOX_AHEAD = 1
VMEM_LIMIT = 56 * 1024 * 1024

BF = jnp.bfloat16
F32 = jnp.float32
NEG_INF = float("-inf")


def _rms(x, g):
    ms = jnp.mean(x * x, axis=-1, keepdims=True)
    return x * lax.rsqrt(ms + RMS_EPS) * g


def _dot(a, b):
    return jnp.dot(a, b, preferred_element_type=F32)


def _dot_nt(a, b):
    return lax.dot_general(a, b, (((1,), (1,)), ((), ())), preferred_element_type=F32)


def _set_rows(full, r0, part):
    return part if r0 == 0 else jnp.concatenate([full[:r0], part], axis=0)


def _rotate(x, c, s1, s2):
    half = ROT_DIM // 2
    return x * c + pltpu.roll(x, LANES - half, 1) * s1 + pltpu.roll(x, half, 1) * s2


def _qkv_kernel(x_ref, g_ref, w_ref, *rest, has_forget, rotary, n_cast):
    n_extra = int(has_forget) + 3 * int(rotary)
    ins, outs = rest[:n_extra + n_cast], rest[n_extra + n_cast:]
    hn = _rms(x_ref[...], g_ref[...]).astype(BF)
    o_ref = outs[0]
    n_tiles = D_ATTN // LANES
    for j in range(3 * D_ATTN // TN):
        y = _dot(hn, w_ref[:, j * TN:(j + 1) * TN])
        if (j + 1) * TN <= D_ATTN:
            y = y * Q_SCALE
        if rotary and (j + 1) * TN <= 2 * D_ATTN:
            c, s1, s2 = (r[...] for r in ins[int(has_forget):n_extra])
            tiles = [y[:, t * LANES:(t + 1) * LANES] for t in range(n_tiles)]
            tiles = [x if t < n_tiles // 2 else _rotate(x, c, s1, s2) for t, x in enumerate(tiles)]
            y = jnp.concatenate(tiles, axis=1)
        o_ref[:, j * TN:(j + 1) * TN] = y.astype(BF)
    if has_forget:
        outs[1][...] = _dot(hn, ins[0][...])
    for src, dst in zip(ins[n_extra:], outs[1 + int(has_forget):]):
        dst[...] = src[...].astype(BF)


def _const_spec(shape, layer=None):
    if layer is None:
        return pl.BlockSpec(shape, lambda *_: (0,) * len(shape), pipeline_mode=pl.Buffered(1))
    return pl.BlockSpec((None,) + shape, lambda *_: (layer,) + (0,) * len(shape),
                        pipeline_mode=pl.Buffered(1))


def _qkv_call(x, g, w, wf=None, layer=None, rot_tables=None, casts=()):
    n = 3 * D_ATTN
    steps = M_TOKENS // TM
    in_specs = [pl.BlockSpec((TM, D_MODEL), lambda i: (i, 0)),
                _const_spec((1, D_MODEL)),
                _const_spec((D_MODEL, n), layer)]
    out_specs = [pl.BlockSpec((TM, n), lambda i: (i, 0))]
    out_shape = [jax.ShapeDtypeStruct((M_TOKENS, n), BF)]
    args = [x, g.reshape(1, D_MODEL), w]
    if wf is not None:
        in_specs.append(_const_spec((D_MODEL, LANES)))
        out_specs.append(pl.BlockSpec((TM, LANES), lambda i: (i, 0)))
        out_shape.append(jax.ShapeDtypeStruct((M_TOKENS, LANES), F32))
        args.append(wf)
    if rot_tables is not None:
        for tab in rot_tables:
            in_specs.append(pl.BlockSpec((TM, LANES), lambda i: (i % (SEQ // TM), 0)))
            args.append(tab)
    for stack, idx in casts:
        _, rows, cols = stack.shape
        hold = 1
        while (rows * hold // steps) % BF16_ROWS:
            hold *= 2
        blk = rows * hold // steps
        in_specs.append(pl.BlockSpec((None, blk, cols), lambda i, idx=idx, hold=hold: (idx, i // hold, 0)))
        out_specs.append(pl.BlockSpec((blk, cols), lambda i, hold=hold: (i // hold, 0)))
        out_shape.append(jax.ShapeDtypeStruct((rows, cols), BF))
        args.append(stack)
    outs = pl.pallas_call(
        functools.partial(_qkv_kernel, has_forget=wf is not None, rotary=rot_tables is not None,
                          n_cast=len(casts)),
        grid=(steps,), in_specs=in_specs, out_specs=out_specs, out_shape=out_shape,
        compiler_params=pltpu.CompilerParams(dimension_semantics=("arbitrary",),
                                             vmem_limit_bytes=VMEM_LIMIT),
        name="norm_qkv" if wf is None else "norm_qkvf")(*args)
    return list(outs)


def _ffn_kernel(x_ref, oa_ref, ob_ref, wo_ref, g_ref, win_ref, wout_ref, gf_ref, out_ref,
                acc_ref, hn_ref, *, final_norm):
    half = D_ATTN // 2
    xn = x_ref[...] + _dot(oa_ref[...], wo_ref[0:half, :]) + _dot(ob_ref[...], wo_ref[half:, :])
    acc_ref[...] = xn
    hn_ref[...] = _rms(xn, g_ref[...]).astype(BF)

    def tile(c0, width):
        hn = hn_ref[...]
        c1 = D_FF + c0
        if not isinstance(c0, int):
            c1 = pl.multiple_of(c1, math.gcd(D_FF, TF))
        g = _dot(hn, win_ref[:, pl.ds(c0, width)])
        u = _dot(hn, win_ref[:, pl.ds(c1, width)])
        a = (g * (1.0 / (1.0 + jnp.exp(-g))) * u).astype(BF)
        acc_ref[...] += _dot(a, wout_ref[pl.ds(c0, width), :])

    def body(f, c):
        tile(pl.multiple_of(f * TF, TF), TF)
        return c

    n_tiles = D_FF // TF
    lax.fori_loop(0, n_tiles, body, 0)
    if D_FF > n_tiles * TF:
        tile(n_tiles * TF, D_FF - n_tiles * TF)
    y = acc_ref[...]
    if final_norm:
        y = _rms(y, gf_ref[...])
    out_ref[...] = y


def _ffn_call(x, oa, ob, ob_col, wo, g, win, wout, gf, final_norm):
    half = D_ATTN // 2
    in_specs = [pl.BlockSpec((TM_FFN, D_MODEL), lambda i: (i, 0)),
                pl.BlockSpec((TM_FFN, half), lambda i: (i, 0)),
                pl.BlockSpec((TM_FFN, half), lambda i: (i, ob_col)),
                _const_spec((D_ATTN, D_MODEL)),
                _const_spec((1, D_MODEL)),
                _const_spec((D_MODEL, 2 * D_FF)),
                _const_spec((D_FF, D_MODEL)),
                _const_spec((1, D_MODEL))]
    return pl.pallas_call(
        functools.partial(_ffn_kernel, final_norm=final_norm),
        grid=(M_TOKENS // TM_FFN,), in_specs=in_specs,
        out_specs=pl.BlockSpec((TM_FFN, D_MODEL), lambda i: (i, 0)),
        out_shape=jax.ShapeDtypeStruct((M_TOKENS, D_MODEL), F32),
        scratch_shapes=[pltpu.VMEM((TM_FFN, D_MODEL), F32), pltpu.VMEM((TM_FFN, D_MODEL), BF)],
        compiler_params=pltpu.CompilerParams(dimension_semantics=("parallel",),
                                             vmem_limit_bytes=VMEM_LIMIT),
        name="oproj_ffn")(x, oa, ob, wo, g.reshape(1, D_MODEL), win, wout, gf.reshape(1, D_MODEL))


def _fprep_kernel(fl_ref, b_ref, o_ref):
    x = fl_ref[0] + b_ref[...]
    lf = jnp.minimum(x, 0.0) - jnp.log1p(jnp.exp(-jnp.abs(x)))
    lft = lf.T
    r = lax.broadcasted_iota(jnp.int32, (TK, TK), 0)
    c = lax.broadcasted_iota(jnp.int32, (TK, TK), 1)
    u = jnp.where(r <= c, 1.0, 0.0).astype(BF)
    carry = jnp.zeros((N_HEADS, 1), F32)
    for cb in range(SEQ // TK):
        blk = lft[0:N_HEADS, cb * TK:(cb + 1) * TK]
        b0 = blk.astype(BF)
        r1 = blk - b0.astype(F32)
        b1 = r1.astype(BF)
        b2 = (r1 - b1.astype(F32)).astype(BF)
        cs = _dot(b0, u) + _dot(b1, u) + _dot(b2, u) + carry
        o_ref[0, :, cb * TK:(cb + 1) * TK] = cs * LOG2E
        carry = cs[:, TK - 1:TK]


def _fprep_call(flog, bias):
    return pl.pallas_call(
        _fprep_kernel, grid=(BATCH,),
        in_specs=[pl.BlockSpec((1, SEQ, LANES), lambda b: (b, 0, 0)),
                  pl.BlockSpec((1, LANES), lambda b: (0, 0))],
        out_specs=pl.BlockSpec((1, N_HEADS, SEQ), lambda b: (b, 0, 0)),
        out_shape=jax.ShapeDtypeStruct((BATCH, N_HEADS, SEQ), F32),
        compiler_params=pltpu.CompilerParams(dimension_semantics=("parallel",),
                                             vmem_limit_bytes=VMEM_LIMIT),
        name="forget_cumsum")(flog, bias)


def _tri_mask(rows, strict, row0=0):
    row = lax.broadcasted_iota(jnp.int32, (rows, TK), 0) + row0
    col = lax.broadcasted_iota(jnp.int32, (rows, TK), 1)
    return (col < row) if strict else (col <= row)


def _head_split(q):
    lane = lax.broadcasted_iota(jnp.int32, q.shape, 1)
    zero = jnp.zeros_like(q)
    return jnp.where(lane < HEAD_DIM, q, zero), jnp.where(lane >= HEAD_DIM, q, zero)


def _fox_kernel(q_ref, k_ref, v_ref, f_ref, o_ref, va_ref, vb_ref):
    pair = pl.program_id(1)
    lane_kv = lax.broadcasted_iota(jnp.int32, (SEQ, LANES), 1)
    v = v_ref[0]
    one = jnp.ones_like(v)
    va_ref[...] = jnp.where(lane_kv < HEAD_DIM, v, one)
    vb_ref[...] = jnp.where(lane_kv >= HEAD_DIM, v, one)
    vxs = (va_ref, vb_ref)
    diag_ok = _tri_mask(RC, False)
    is_a = lax.broadcasted_iota(jnp.int32, (RC, LANES), 1) < HEAD_DIM

    def qk(c):
        qh = _head_split(q_ref[0, c * RC:(c + 1) * RC, :])
        k = k_ref[0, 0:(c + 1) * RC, :]
        return [_dot_nt(qh[hh], k) for hh in (0, 1)]

    def softmax(c, zs):
        kend = (c + 1) * RC
        ps = []
        for hh in (0, 1):
            s = zs[hh] - f_ref[0, pl.ds(2 * pair + hh, 1), 0:kend]
            tail = jnp.where(diag_ok, s[:, kend - RC:], NEG_INF)
            s = tail if c == 0 else jnp.concatenate([s[:, :kend - RC], tail], axis=1)
            ps.append(jnp.exp2(s - jnp.max(s, axis=-1, keepdims=True)).astype(BF))
        return ps

    def pv(c, ps):
        outs = []
        for hh in (0, 1):
            o = _dot(ps[hh], vxs[hh][0:(c + 1) * RC, :])
            outs.append(o / pltpu.roll(o, HEAD_DIM, 1))
        o_ref[0, c * RC:(c + 1) * RC, :] = jnp.where(is_a, outs[0], outs[1]).astype(BF)

    n_chunks = SEQ // RC
    zs = {c: qk(c) for c in range(min(FOX_AHEAD, n_chunks))}
    for c in range(n_chunks):
        if c + FOX_AHEAD < n_chunks:
            zs[c + FOX_AHEAD] = qk(c + FOX_AHEAD)
        pv(c, softmax(c, zs.pop(c)))


def _fox_call(qkv3, f_row):
    n_pairs = N_HEADS // 2
    return pl.pallas_call(
        _fox_kernel, grid=(BATCH, n_pairs),
        in_specs=[pl.BlockSpec((1, SEQ, LANES), lambda b, p: (b, 0, p)),
                  pl.BlockSpec((1, SEQ, LANES), lambda b, p: (b, 0, n_pairs + p)),
                  pl.BlockSpec((1, SEQ, LANES), lambda b, p: (b, 0, 2 * n_pairs + p)),
                  pl.BlockSpec((1, N_HEADS, SEQ), lambda b, p: (b, 0, 0))],
        out_specs=pl.BlockSpec((1, SEQ, LANES), lambda b, p: (b, 0, p)),
        out_shape=jax.ShapeDtypeStruct((BATCH, SEQ, D_ATTN), BF),
        scratch_shapes=[pltpu.VMEM((SEQ, LANES), BF), pltpu.VMEM((SEQ, LANES), BF)],
        compiler_params=pltpu.CompilerParams(
            dimension_semantics=("parallel", "arbitrary"),
            vmem_limit_bytes=VMEM_LIMIT),
        name="fox_attn")(qkv3, qkv3, qkv3, f_row)


def _sb_kernel(q_ref, k_ref, v_ref, o_ref, u_ref):
    r = lax.broadcasted_iota(jnp.int32, (TK, TK), 0)
    c = lax.broadcasted_iota(jnp.int32, (TK, TK), 1)
    u_ref[...] = jnp.where(r > c, 1.0, 0.0)
    strict = _tri_mask(RC, True)
    lane = lax.broadcasted_iota(jnp.int32, (RC, LANES), 1)
    heads = (0, 1)

    def qk(c):
        qh = _head_split(q_ref[0, c * RC:(c + 1) * RC, :])
        z = _dot_nt(jnp.concatenate(qh, axis=0), k_ref[0, 0:(c + 1) * RC, :])
        return z[:RC], z[RC:]

    def gates(c, zs):
        nb = c + 1
        parts = []
        for z in zs:
            nz = -z
            lg = jnp.minimum(nz, 0.0) - jnp.log2(1.0 + jnp.exp2(jnp.minimum(z, nz)))
            arg = z + lg
            blocks = [lg[:, j * TK:(j + 1) * TK] for j in range(nb)]
            blocks[-1] = jnp.where(strict, blocks[-1], 0.0)
            carries, run = [None] * nb, None
            for j in reversed(range(nb)):
                carries[j] = run
                rs = jnp.sum(blocks[j], axis=-1, keepdims=True)
                run = rs if run is None else run + rs
            parts.append((arg, blocks, carries))
        suf = _dot(jnp.concatenate(parts[0][1] + parts[1][1], axis=0), u_ref[...])
        half = nb * RC
        return [(parts[hh][0], suf[hh * half:(hh + 1) * half], parts[hh][2]) for hh in heads]

    def values(c, gs):
        nb = c + 1
        weights = []
        for arg, suf, carries in gs:
            ws = []
            for j in range(nb):
                e = suf[j * RC:(j + 1) * RC]
                if carries[j] is not None:
                    e = e + carries[j]
                ws.append(jnp.exp2(arg[:, j * TK:(j + 1) * TK] + e))
            ws[-1] = jnp.where(strict, ws[-1], 0.0)
            weights.append((ws[0] if nb == 1 else jnp.concatenate(ws, axis=1)).astype(BF))
        o = _dot(jnp.concatenate(weights, axis=0), v_ref[0, 0:nb * TK, :])
        o_ref[0, c * RC:(c + 1) * RC, :] = jnp.where(lane < HEAD_DIM, o[:RC], o[RC:]).astype(BF)

    n_chunks = SEQ // RC
    zs = {0: qk(0)}
    gs = {0: gates(0, zs.pop(0))}
    for c in range(n_chunks):
        if c + 1 < n_chunks:
            zs[c + 1] = qk(c + 1)
            gs[c + 1] = gates(c + 1, zs.pop(c + 1))
        values(c, gs.pop(c))


def _sb_call(qkv3):
    n_pairs = N_HEADS // 2
    n_sb = n_pairs // 2
    return pl.pallas_call(
        _sb_kernel, grid=(BATCH, n_sb),
        in_specs=[pl.BlockSpec((1, SEQ, LANES), lambda b, p: (b, 0, p)),
                  pl.BlockSpec((1, SEQ, LANES), lambda b, p: (b, 0, n_pairs + p)),
                  pl.BlockSpec((1, SEQ, LANES), lambda b, p: (b, 0, 2 * n_pairs + p))],
        out_specs=pl.BlockSpec((1, SEQ, LANES), lambda b, p: (b, 0, p)),
        out_shape=jax.ShapeDtypeStruct((BATCH, SEQ, D_ATTN // 2), BF),
        scratch_shapes=[pltpu.VMEM((TK, TK), F32)],
        compiler_params=pltpu.CompilerParams(
            dimension_semantics=("parallel", "arbitrary"),
            vmem_limit_bytes=VMEM_LIMIT),
        name="stickbreak_attn")(qkv3, qkv3, qkv3)


def _dil_kernel(q_ref, k_ref, v_ref, o_ref,
                nat_ref, qp_ref, kp_ref, vp_ref, xo_ref, xl_ref, on_ref):
    lane = lax.broadcasted_iota(jnp.int32, (DB, LANES), 1)
    is_a = lane < HEAD_DIM

    for t in range(N_DB):
        rows = slice(t * DB, (t + 1) * DB)
        nat_ref[0, rows, :] = q_ref[0, rows, :].astype(F32)
        nat_ref[1, rows, :] = k_ref[0, rows, :].astype(F32)
        nat_ref[2, rows, :] = v_ref[0, rows, :].astype(F32)

    for pi, (window, d) in enumerate(DIL_PATTERNS):
        per_res = SEQ // d // DB
        for t in range(N_DB):
            r, c = divmod(t, per_res)
            src = pl.ds(r + d * DB * c, DB, stride=d) if d > 1 else pl.ds(t * DB, DB)
            dst = slice(t * DB, (t + 1) * DB)
            qp_ref[pi, dst, :] = nat_ref[0, src, :].astype(BF)
            kp_ref[pi, dst, :] = nat_ref[1, src, :].astype(BF)
            vp_ref[pi, dst, :] = nat_ref[2, src, :].astype(BF)

    row = lax.broadcasted_iota(jnp.int32, (DB, DB), 0)
    col = lax.broadcasted_iota(jnp.int32, (DB, DB), 1)
    win_ok = jnp.concatenate([col >= row, col <= row], axis=1)
    cur_ok = col <= row
    ones = jnp.ones((2 * DB, LANES), BF)

    def window(t, pi):
        per_res = SEQ // DIL_PATTERNS[pi][1] // DB
        has_prev = t % per_res != 0
        return slice((t - 1 if has_prev else t) * DB, (t + 1) * DB)

    for g0 in range(0, N_DB, DIL_GROUP):
        items = [(t, pi) for t in range(g0, g0 + DIL_GROUP) for pi in range(len(DIL_PATTERNS))]
        chains = [(t, pi, hh) for t, pi in items for hh in (0, 1)]
        scores = {}
        for t, pi in items:
            qb = qp_ref[pi, t * DB:(t + 1) * DB, :]
            kw = kp_ref[pi, window(t, pi), :]
            qa, qbb = _head_split(qb)
            scores[(t, pi, 0)] = _dot_nt(qa, kw)
            scores[(t, pi, 1)] = _dot_nt(qbb, kw)
        probs, maxes = {}, {}
        for key in chains:
            s = scores.pop(key)
            s = jnp.where(win_ok if s.shape[1] == 2 * DB else cur_ok, s, NEG_INF)
            m = jnp.max(s, axis=-1, keepdims=True)
            probs[key] = jnp.exp2(s - m).astype(BF)
            maxes[key] = m
        nums, dens = {}, {}
        for t, pi, hh in chains:
            p = probs.pop((t, pi, hh))
            nums[(t, pi, hh)] = _dot(p, vp_ref[pi, window(t, pi), :])
            dens[(t, pi, hh)] = _dot(p, ones[:p.shape[1]])
        for t, pi in items:
            ka, kb = (t, pi, 0), (t, pi, 1)
            den = jnp.where(is_a, dens.pop(ka), dens.pop(kb))
            rows = slice(t * DB, (t + 1) * DB)
            xo_ref[pi, rows, :] = jnp.where(is_a, nums.pop(ka), nums.pop(kb)) / den
            xl_ref[pi, rows, :] = jnp.where(is_a, maxes.pop(ka), maxes.pop(kb)) + jnp.log2(den)

    d_max = DIL_PATTERNS[-1][1]
    for r in range(d_max):
        xs, ls = [], []
        for pi, (window_len, d) in enumerate(DIL_PATTERNS):
            step = d_max // d
            start = (r % d) * (SEQ // d) + r // d
            rows = pl.ds(start, DB, stride=step) if step > 1 else pl.ds(start, DB)
            xs.append(xo_ref[pi, rows, :])
            ls.append(xl_ref[pi, rows, :])
        mx = functools.reduce(jnp.maximum, ls)
        es = [jnp.exp2(l - mx) for l in ls]
        inv = 1.0 / functools.reduce(lambda a, b: a + b, es)
        on_ref[pl.ds(r, DB, stride=d_max), :] = functools.reduce(
            lambda a, b: a + b, [(e * inv) * x for e, x in zip(es, xs)])

    o_ref[0] = on_ref[...].astype(BF)


def _dil_call(qkv3):
    n_pairs = N_HEADS // 2
    n_sb = n_pairs // 2
    n_pat = len(DIL_PATTERNS)
    return pl.pallas_call(
        _dil_kernel, grid=(BATCH, n_pairs - n_sb),
        in_specs=[pl.BlockSpec((1, SEQ, LANES), lambda b, p: (b, 0, n_sb + p)),
                  pl.BlockSpec((1, SEQ, LANES), lambda b, p: (b, 0, n_pairs + n_sb + p)),
                  pl.BlockSpec((1, SEQ, LANES), lambda b, p: (b, 0, 2 * n_pairs + n_sb + p))],
        out_specs=pl.BlockSpec((1, SEQ, LANES), lambda b, p: (b, 0, p)),
        out_shape=jax.ShapeDtypeStruct((BATCH, SEQ, D_ATTN // 2), BF),
        scratch_shapes=[pltpu.VMEM((3, SEQ, LANES), F32),
                        pltpu.VMEM((n_pat, SEQ, LANES), BF),
                        pltpu.VMEM((n_pat, SEQ, LANES), BF),
                        pltpu.VMEM((n_pat, SEQ, LANES), BF),
                        pltpu.VMEM((n_pat, SEQ, LANES), F32),
                        pltpu.VMEM((n_pat, SEQ, LANES), F32),
                        pltpu.VMEM((SEQ, LANES), F32)],
        compiler_params=pltpu.CompilerParams(dimension_semantics=("parallel", "arbitrary"),
                                             vmem_limit_bytes=VMEM_LIMIT),
        name="dilated_attn")(qkv3, qkv3, qkv3)


def _rotary_tables():
    half = ROT_DIM // 2
    pos = jnp.arange(SEQ, dtype=F32)
    inv_freq = ROPE_THETA ** (-jnp.arange(half, dtype=F32) * 2.0 / ROT_DIM)
    ang = pos[:, None] * inv_freq[None, :]
    cos, sin = jnp.cos(ang), jnp.sin(ang)
    zeros = jnp.zeros((SEQ, HEAD_DIM - ROT_DIM), F32)
    z8 = jnp.zeros((SEQ, half), F32)
    c_head = jnp.concatenate([cos, cos, zeros + 1.0], axis=1)
    s1_head = jnp.concatenate([-sin, z8, zeros], axis=1)
    s2_head = jnp.concatenate([z8, sin, zeros], axis=1)
    two = lambda t: jnp.concatenate([t, t], axis=1)
    return two(c_head), two(s1_head), two(s2_head)


def kernel(x, norm_mix, w_qkv_even, w_o_even, w_qkvf_odd, b_forget, w_o_odd, norm_ffn,
           w_ffn_in, w_ffn_out, norm_final):
    h = x.reshape(M_TOKENS, D_MODEL)
    cos_t, sin1_t, sin2_t = _rotary_tables()
    n_qkv = 3 * D_ATTN
    w_qkv_even = w_qkv_even.astype(BF)
    for layer in range(DEPTH):
        i = layer // 2
        casts = [(w_o_even if layer % 2 == 0 else w_o_odd, i), (w_ffn_in, layer), (w_ffn_out, layer)]
        if layer % 2 == 0:
            qkv, wo, win, wout = _qkv_call(h, norm_mix[layer], w_qkv_even, layer=i,
                                           rot_tables=(cos_t, sin1_t, sin2_t), casts=casts)
            qkv3 = qkv.reshape(BATCH, SEQ, n_qkv)
            oa = _sb_call(qkv3).reshape(M_TOKENS, D_ATTN // 2)
            ob = _dil_call(qkv3).reshape(M_TOKENS, D_ATTN // 2)
            ob_col = 0
        else:
            w = w_qkvf_odd[i]
            wf = jnp.pad(w[:, n_qkv:], ((0, 0), (0, LANES - N_HEADS))).astype(BF)
            qkv, flog, wo, win, wout = _qkv_call(h, norm_mix[layer], w[:, :n_qkv].astype(BF), wf,
                                                 casts=casts)
            bias = jnp.pad(b_forget[i], (0, LANES - N_HEADS)).reshape(1, LANES)
            f_row = _fprep_call(flog.reshape(BATCH, SEQ, LANES), bias)
            o = _fox_call(qkv.reshape(BATCH, SEQ, n_qkv), f_row).reshape(M_TOKENS, D_ATTN)
            oa, ob, ob_col = o, o, 1
        h = _ffn_call(h, oa, ob, ob_col, wo, norm_ffn[layer], win, wout, norm_final,
                      final_norm=(layer == DEPTH - 1))
    return h.reshape(BATCH, SEQ, D_MODEL)
```

```python
import functools
import math

import jax
import jax.numpy as jnp
from jax import lax
from jax.experimental import pallas as pl
from jax.experimental.pallas import tpu as pltpu

D_MODEL = 1024
BATCH = 8
SEQ = 2048
DEPTH = 4
HEAD_DIM = 64
N_HEADS = 16
D_ATTN = N_HEADS * HEAD_DIM
D_FF = 2816
ROPE_THETA = 500000.0
ROT_DIM = HEAD_DIM // 4
DIL_PATTERNS = ((128, 1), (512, 4), (2048, 16))
RMS_EPS = 1e-5
LOG2E = math.log2(math.e)
Q_SCALE = HEAD_DIM ** -0.5 * LOG2E

LANES = 128
BF16_ROWS = 16
M_TOKENS = BATCH * SEQ
TM = 1024
TM_FFN = 1024
TN = 1024
TF = 512
FFN_PER_TRIP = 2
TQ = 1024
TK = 256
RC = 256
DB = 128
N_DB = SEQ // DB
DIL_GROUP = 8
FOX_AHEAD = 1
VMEM_LIMIT = 56 * 1024 * 1024

BF = jnp.bfloat16
F32 = jnp.float32
NEG_INF = float("-inf")


def _rms(x, g):
    ms = jnp.mean(x * x, axis=-1, keepdims=True)
    return x * lax.rsqrt(ms + RMS_EPS) * g


def _dot(a, b):
    return jnp.dot(a, b, preferred_element_type=F32)


def _dot_nt(a, b):
    return lax.dot_general(a, b, (((1,), (1,)), ((), ())), preferred_element_type=F32)


def _set_rows(full, r0, part):
    return part if r0 == 0 else jnp.concatenate([full[:r0], part], axis=0)


def _rotate(x, c, s1, s2):
    half = ROT_DIM // 2
    return x * c + pltpu.roll(x, LANES - half, 1) * s1 + pltpu.roll(x, half, 1) * s2


def _qkv_kernel(x_ref, g_ref, w_ref, *rest, has_forget, rotary, n_cast):
    n_extra = int(has_forget) + 3 * int(rotary)
    ins, outs = rest[:n_extra + n_cast], rest[n_extra + n_cast:]
    hn = _rms(x_ref[...], g_ref[...]).astype(BF)
    o_ref = outs[0]
    n_tiles = D_ATTN // LANES
    for j in range(3 * D_ATTN // TN):
        y = _dot(hn, w_ref[:, j * TN:(j + 1) * TN])
        if (j + 1) * TN <= D_ATTN:
            y = y * Q_SCALE
        if rotary and (j + 1) * TN <= 2 * D_ATTN:
            c, s1, s2 = (r[...] for r in ins[int(has_forget):n_extra])
            tiles = [y[:, t * LANES:(t + 1) * LANES] for t in range(n_tiles)]
            tiles = [x if t < n_tiles // 2 else _rotate(x, c, s1, s2) for t, x in enumerate(tiles)]
            y = jnp.concatenate(tiles, axis=1)
        o_ref[:, j * TN:(j + 1) * TN] = y.astype(BF)
    if has_forget:
        outs[1][...] = _dot(hn, ins[0][...])
    for src, dst in zip(ins[n_extra:], outs[1 + int(has_forget):]):
        dst[...] = src[...].astype(BF)


def _const_spec(shape, layer=None):
    if layer is None:
        return pl.BlockSpec(shape, lambda *_: (0,) * len(shape), pipeline_mode=pl.Buffered(1))
    return pl.BlockSpec((None,) + shape, lambda *_: (layer,) + (0,) * len(shape),
                        pipeline_mode=pl.Buffered(1))


def _qkv_call(x, g, w, wf=None, layer=None, rot_tables=None, casts=()):
    n = 3 * D_ATTN
    steps = M_TOKENS // TM
    in_specs = [pl.BlockSpec((TM, D_MODEL), lambda i: (i, 0)),
                _const_spec((1, D_MODEL)),
                _const_spec((D_MODEL, n), layer)]
    out_specs = [pl.BlockSpec((TM, n), lambda i: (i, 0))]
    out_shape = [jax.ShapeDtypeStruct((M_TOKENS, n), BF)]
    args = [x, g.reshape(1, D_MODEL), w]
    if wf is not None:
        in_specs.append(_const_spec((D_MODEL, LANES)))
        out_specs.append(pl.BlockSpec((TM, LANES), lambda i: (i, 0)))
        out_shape.append(jax.ShapeDtypeStruct((M_TOKENS, LANES), F32))
        args.append(wf)
    if rot_tables is not None:
        for tab in rot_tables:
            in_specs.append(pl.BlockSpec((TM, LANES), lambda i: (i % (SEQ // TM), 0)))
            args.append(tab)
    for stack, idx in casts:
        _, rows, cols = stack.shape
        hold = 1
        while (rows * hold // steps) % BF16_ROWS:
            hold *= 2
        blk = rows * hold // steps
        in_specs.append(pl.BlockSpec((None, blk, cols), lambda i, idx=idx, hold=hold: (idx, i // hold, 0)))
        out_specs.append(pl.BlockSpec((blk, cols), lambda i, hold=hold: (i // hold, 0)))
        out_shape.append(jax.ShapeDtypeStruct((rows, cols), BF))
        args.append(stack)
    outs = pl.pallas_call(
        functools.partial(_qkv_kernel, has_forget=wf is not None, rotary=rot_tables is not None,
                          n_cast=len(casts)),
        grid=(steps,), in_specs=in_specs, out_specs=out_specs, out_shape=out_shape,
        compiler_params=pltpu.CompilerParams(dimension_semantics=("arbitrary",),
                                             vmem_limit_bytes=VMEM_LIMIT),
        name="norm_qkv" if wf is None else "norm_qkvf")(*args)
    return list(outs)


def _ffn_kernel(x_ref, oa_ref, ob_ref, wo_ref, g_ref, win_ref, wout_ref, gf_ref, out_ref,
                acc_ref, hn_ref, *, final_norm):
    half = D_ATTN // 2
    xn = x_ref[...] + _dot(oa_ref[...], wo_ref[0:half, :]) + _dot(ob_ref[...], wo_ref[half:, :])
    acc_ref[...] = xn
    hn_ref[...] = _rms(xn, g_ref[...]).astype(BF)

    def act(c0, width):
        hn = hn_ref[...]
        c1 = D_FF + c0
        if not isinstance(c0, int):
            c1 = pl.multiple_of(c1, math.gcd(D_FF, TF))
        g = _dot(hn, win_ref[:, pl.ds(c0, width)])
        u = _dot(hn, win_ref[:, pl.ds(c1, width)])
        return (g * (1.0 / (1.0 + jnp.exp(-g))) * u).astype(BF)

    def tiles(spans):
        acts = [act(c0, width) for c0, width in spans]
        down = [_dot(a, wout_ref[pl.ds(c0, width), :]) for a, (c0, width) in zip(acts, spans)]
        acc_ref[...] += functools.reduce(lambda p, q: p + q, down)

    def body(f, c):
        c0 = pl.multiple_of(f * (FFN_PER_TRIP * TF), TF)
        tiles([(c0 + j * TF, TF) for j in range(FFN_PER_TRIP)])
        return c

    n_trips = D_FF // (FFN_PER_TRIP * TF)
    lax.fori_loop(0, n_trips, body, 0)
    rest = list(range(n_trips * FFN_PER_TRIP * TF, D_FF, TF))
    if rest:
        tiles([(c0, min(TF, D_FF - c0)) for c0 in rest])
    y = acc_ref[...]
    if final_norm:
        y = _rms(y, gf_ref[...])
    out_ref[...] = y


def _ffn_call(x, oa, ob, ob_col, wo, g, win, wout, gf, final_norm):
    half = D_ATTN // 2
    in_specs = [pl.BlockSpec((TM_FFN, D_MODEL), lambda i: (i, 0)),
                pl.BlockSpec((TM_FFN, half), lambda i: (i, 0)),
                pl.BlockSpec((TM_FFN, half), lambda i: (i, ob_col)),
                _const_spec((D_ATTN, D_MODEL)),
                _const_spec((1, D_MODEL)),
                _const_spec((D_MODEL, 2 * D_FF)),
                _const_spec((D_FF, D_MODEL)),
                _const_spec((1, D_MODEL))]
    return pl.pallas_call(
        functools.partial(_ffn_kernel, final_norm=final_norm),
        grid=(M_TOKENS // TM_FFN,), in_specs=in_specs,
        out_specs=pl.BlockSpec((TM_FFN, D_MODEL), lambda i: (i, 0)),
        out_shape=jax.ShapeDtypeStruct((M_TOKENS, D_MODEL), F32),
        scratch_shapes=[pltpu.VMEM((TM_FFN, D_MODEL), F32), pltpu.VMEM((TM_FFN, D_MODEL), BF)],
        compiler_params=pltpu.CompilerParams(dimension_semantics=("parallel",),
                                             vmem_limit_bytes=VMEM_LIMIT),
        name="oproj_ffn")(x, oa, ob, wo, g.reshape(1, D_MODEL), win, wout, gf.reshape(1, D_MODEL))


def _fprep_kernel(fl_ref, b_ref, o_ref):
    x = fl_ref[0] + b_ref[...]
    lf = jnp.minimum(x, 0.0) - jnp.log1p(jnp.exp(-jnp.abs(x)))
    lft = lf.T
    r = lax.broadcasted_iota(jnp.int32, (TK, TK), 0)
    c = lax.broadcasted_iota(jnp.int32, (TK, TK), 1)
    u = jnp.where(r <= c, 1.0, 0.0).astype(BF)
    carry = jnp.zeros((N_HEADS, 1), F32)
    for cb in range(SEQ // TK):
        blk = lft[0:N_HEADS, cb * TK:(cb + 1) * TK]
        b0 = blk.astype(BF)
        r1 = blk - b0.astype(F32)
        b1 = r1.astype(BF)
        b2 = (r1 - b1.astype(F32)).astype(BF)
        cs = _dot(b0, u) + _dot(b1, u) + _dot(b2, u) + carry
        o_ref[0, :, cb * TK:(cb + 1) * TK] = cs * LOG2E
        carry = cs[:, TK - 1:TK]


def _fprep_call(flog, bias):
    return pl.pallas_call(
        _fprep_kernel, grid=(BATCH,),
        in_specs=[pl.BlockSpec((1, SEQ, LANES), lambda b: (b, 0, 0)),
                  pl.BlockSpec((1, LANES), lambda b: (0, 0))],
        out_specs=pl.BlockSpec((1, N_HEADS, SEQ), lambda b: (b, 0, 0)),
        out_shape=jax.ShapeDtypeStruct((BATCH, N_HEADS, SEQ), F32),
        compiler_params=pltpu.CompilerParams(dimension_semantics=("parallel",),
                                             vmem_limit_bytes=VMEM_LIMIT),
        name="forget_cumsum")(flog, bias)


def _tri_mask(rows, strict, row0=0):
    row = lax.broadcasted_iota(jnp.int32, (rows, TK), 0) + row0
    col = lax.broadcasted_iota(jnp.int32, (rows, TK), 1)
    return (col < row) if strict else (col <= row)


def _head_split(q):
    lane = lax.broadcasted_iota(jnp.int32, q.shape, 1)
    zero = jnp.zeros_like(q)
    return jnp.where(lane < HEAD_DIM, q, zero), jnp.where(lane >= HEAD_DIM, q, zero)


def _fox_kernel(q_ref, k_ref, v_ref, f_ref, o_ref, va_ref, vb_ref):
    pair = pl.program_id(1)
    lane_kv = lax.broadcasted_iota(jnp.int32, (SEQ, LANES), 1)
    v = v_ref[0]
    one = jnp.ones_like(v)
    va_ref[...] = jnp.where(lane_kv < HEAD_DIM, v, one)
    vb_ref[...] = jnp.where(lane_kv >= HEAD_DIM, v, one)
    vxs = (va_ref, vb_ref)
    diag_ok = _tri_mask(RC, False)
    is_a = lax.broadcasted_iota(jnp.int32, (RC, LANES), 1) < HEAD_DIM

    def qk(c):
        qh = _head_split(q_ref[0, c * RC:(c + 1) * RC, :])
        k = k_ref[0, 0:(c + 1) * RC, :]
        return [_dot_nt(qh[hh], k) for hh in (0, 1)]

    def softmax(c, zs):
        kend = (c + 1) * RC
        ps = []
        for hh in (0, 1):
            s = zs[hh] - f_ref[0, pl.ds(2 * pair + hh, 1), 0:kend]
            tail = jnp.where(diag_ok, s[:, kend - RC:], NEG_INF)
            s = tail if c == 0 else jnp.concatenate([s[:, :kend - RC], tail], axis=1)
            ps.append(jnp.exp2(s - jnp.max(s, axis=-1, keepdims=True)).astype(BF))
        return ps

    def pv(c, ps):
        outs = []
        for hh in (0, 1):
            o = _dot(ps[hh], vxs[hh][0:(c + 1) * RC, :])
            outs.append(o / pltpu.roll(o, HEAD_DIM, 1))
        o_ref[0, c * RC:(c + 1) * RC, :] = jnp.where(is_a, outs[0], outs[1]).astype(BF)

    n_chunks = SEQ // RC
    zs = {c: qk(c) for c in range(min(FOX_AHEAD, n_chunks))}
    for c in range(n_chunks):
        if c + FOX_AHEAD < n_chunks:
            zs[c + FOX_AHEAD] = qk(c + FOX_AHEAD)
        pv(c, softmax(c, zs.pop(c)))


def _fox_call(qkv3, f_row):
    n_pairs = N_HEADS // 2
    return pl.pallas_call(
        _fox_kernel, grid=(BATCH, n_pairs),
        in_specs=[pl.BlockSpec((1, SEQ, LANES), lambda b, p: (b, 0, p)),
                  pl.BlockSpec((1, SEQ, LANES), lambda b, p: (b, 0, n_pairs + p)),
                  pl.BlockSpec((1, SEQ, LANES), lambda b, p: (b, 0, 2 * n_pairs + p)),
                  pl.BlockSpec((1, N_HEADS, SEQ), lambda b, p: (b, 0, 0))],
        out_specs=pl.BlockSpec((1, SEQ, LANES), lambda b, p: (b, 0, p)),
        out_shape=jax.ShapeDtypeStruct((BATCH, SEQ, D_ATTN), BF),
        scratch_shapes=[pltpu.VMEM((SEQ, LANES), BF), pltpu.VMEM((SEQ, LANES), BF)],
        compiler_params=pltpu.CompilerParams(
            dimension_semantics=("parallel", "arbitrary"),
            vmem_limit_bytes=VMEM_LIMIT),
        name="fox_attn")(qkv3, qkv3, qkv3, f_row)


def _sb_kernel(q_ref, k_ref, v_ref, o_ref, u_ref):
    r = lax.broadcasted_iota(jnp.int32, (TK, TK), 0)
    c = lax.broadcasted_iota(jnp.int32, (TK, TK), 1)
    u_ref[...] = jnp.where(r > c, 1.0, 0.0)
    strict = _tri_mask(RC, True)
    lane = lax.broadcasted_iota(jnp.int32, (RC, LANES), 1)
    heads = (0, 1)

    def qk(c):
        qh = _head_split(q_ref[0, c * RC:(c + 1) * RC, :])
        z = _dot_nt(jnp.concatenate(qh, axis=0), k_ref[0, 0:(c + 1) * RC, :])
        return z[:RC], z[RC:]

    def gates(c, zs):
        nb = c + 1
        parts = []
        for z in zs:
            nz = -z
            lg = jnp.minimum(nz, 0.0) - jnp.log2(1.0 + jnp.exp2(jnp.minimum(z, nz)))
            arg = z + lg
            blocks = [lg[:, j * TK:(j + 1) * TK] for j in range(nb)]
            blocks[-1] = jnp.where(strict, blocks[-1], 0.0)
            carries, run = [None] * nb, None
            for j in reversed(range(nb)):
                carries[j] = run
                rs = jnp.sum(blocks[j], axis=-1, keepdims=True)
                run = rs if run is None else run + rs
            parts.append((arg, blocks, carries))
        suf = _dot(jnp.concatenate(parts[0][1] + parts[1][1], axis=0), u_ref[...])
        half = nb * RC
        return [(parts[hh][0], suf[hh * half:(hh + 1) * half], parts[hh][2]) for hh in heads]

    def values(c, gs):
        nb = c + 1
        weights = []
        for arg, suf, carries in gs:
            ws = []
            for j in range(nb):
                e = suf[j * RC:(j + 1) * RC]
                if carries[j] is not None:
                    e = e + carries[j]
                ws.append(jnp.exp2(arg[:, j * TK:(j + 1) * TK] + e))
            ws[-1] = jnp.where(strict, ws[-1], 0.0)
            weights.append((ws[0] if nb == 1 else jnp.concatenate(ws, axis=1)).astype(BF))
        o = _dot(jnp.concatenate(weights, axis=0), v_ref[0, 0:nb * TK, :])
        o_ref[0, c * RC:(c + 1) * RC, :] = jnp.where(lane < HEAD_DIM, o[:RC], o[RC:]).astype(BF)

    n_chunks = SEQ // RC
    zs = {0: qk(0)}
    gs = {0: gates(0, zs.pop(0))}
    for c in range(n_chunks):
        if c + 1 < n_chunks:
            zs[c + 1] = qk(c + 1)
            gs[c + 1] = gates(c + 1, zs.pop(c + 1))
        values(c, gs.pop(c))


def _sb_call(qkv3):
    n_pairs = N_HEADS // 2
    n_sb = n_pairs // 2
    return pl.pallas_call(
        _sb_kernel, grid=(BATCH, n_sb),
        in_specs=[pl.BlockSpec((1, SEQ, LANES), lambda b, p: (b, 0, p)),
                  pl.BlockSpec((1, SEQ, LANES), lambda b, p: (b, 0, n_pairs + p)),
                  pl.BlockSpec((1, SEQ, LANES), lambda b, p: (b, 0, 2 * n_pairs + p))],
        out_specs=pl.BlockSpec((1, SEQ, LANES), lambda b, p: (b, 0, p)),
        out_shape=jax.ShapeDtypeStruct((BATCH, SEQ, D_ATTN // 2), BF),
        scratch_shapes=[pltpu.VMEM((TK, TK), F32)],
        compiler_params=pltpu.CompilerParams(
            dimension_semantics=("parallel", "arbitrary"),
            vmem_limit_bytes=VMEM_LIMIT),
        name="stickbreak_attn")(qkv3, qkv3, qkv3)


def _dil_kernel(q_ref, k_ref, v_ref, o_ref,
                nat_ref, qp_ref, kp_ref, vp_ref, xo_ref, xl_ref, on_ref):
    lane = lax.broadcasted_iota(jnp.int32, (DB, LANES), 1)
    is_a = lane < HEAD_DIM

    for t in range(N_DB):
        rows = slice(t * DB, (t + 1) * DB)
        nat_ref[0, rows, :] = q_ref[0, rows, :].astype(F32)
        nat_ref[1, rows, :] = k_ref[0, rows, :].astype(F32)
        nat_ref[2, rows, :] = v_ref[0, rows, :].astype(F32)

    for pi, (window, d) in enumerate(DIL_PATTERNS):
        per_res = SEQ // d // DB
        for t in range(N_DB):
            r, c = divmod(t, per_res)
            src = pl.ds(r + d * DB * c, DB, stride=d) if d > 1 else pl.ds(t * DB, DB)
            dst = slice(t * DB, (t + 1) * DB)
            qp_ref[pi, dst, :] = nat_ref[0, src, :].astype(BF)
            kp_ref[pi, dst, :] = nat_ref[1, src, :].astype(BF)
            vp_ref[pi, dst, :] = nat_ref[2, src, :].astype(BF)

    row = lax.broadcasted_iota(jnp.int32, (DB, DB), 0)
    col = lax.broadcasted_iota(jnp.int32, (DB, DB), 1)
    win_ok = jnp.concatenate([col >= row, col <= row], axis=1)
    cur_ok = col <= row
    ones = jnp.ones((2 * DB, LANES), BF)

    def window(t, pi):
        per_res = SEQ // DIL_PATTERNS[pi][1] // DB
        has_prev = t % per_res != 0
        return slice((t - 1 if has_prev else t) * DB, (t + 1) * DB)

    for g0 in range(0, N_DB, DIL_GROUP):
        items = [(t, pi) for t in range(g0, g0 + DIL_GROUP) for pi in range(len(DIL_PATTERNS))]
        chains = [(t, pi, hh) for t, pi in items for hh in (0, 1)]
        scores = {}
        for t, pi in items:
            qb = qp_ref[pi, t * DB:(t + 1) * DB, :]
            kw = kp_ref[pi, window(t, pi), :]
            qa, qbb = _head_split(qb)
            scores[(t, pi, 0)] = _dot_nt(qa, kw)
            scores[(t, pi, 1)] = _dot_nt(qbb, kw)
        probs, maxes = {}, {}
        for key in chains:
            s = scores.pop(key)
            s = jnp.where(win_ok if s.shape[1] == 2 * DB else cur_ok, s, NEG_INF)
            m = jnp.max(s, axis=-1, keepdims=True)
            probs[key] = jnp.exp2(s - m).astype(BF)
            maxes[key] = m
        nums, dens = {}, {}
        for t, pi, hh in chains:
            p = probs.pop((t, pi, hh))
            nums[(t, pi, hh)] = _dot(p, vp_ref[pi, window(t, pi), :])
            dens[(t, pi, hh)] = _dot(p, ones[:p.shape[1]])
        for t, pi in items:
            ka, kb = (t, pi, 0), (t, pi, 1)
            den = jnp.where(is_a, dens.pop(ka), dens.pop(kb))
            rows = slice(t * DB, (t + 1) * DB)
            xo_ref[pi, rows, :] = jnp.where(is_a, nums.pop(ka), nums.pop(kb)) / den
            xl_ref[pi, rows, :] = jnp.where(is_a, maxes.pop(ka), maxes.pop(kb)) + jnp.log2(den)

    d_max = DIL_PATTERNS[-1][1]
    for r in range(d_max):
        xs, ls = [], []
        for pi, (window_len, d) in enumerate(DIL_PATTERNS):
            step = d_max // d
            start = (r % d) * (SEQ // d) + r // d
            rows = pl.ds(start, DB, stride=step) if step > 1 else pl.ds(start, DB)
            xs.append(xo_ref[pi, rows, :])
            ls.append(xl_ref[pi, rows, :])
        mx = functools.reduce(jnp.maximum, ls)
        es = [jnp.exp2(l - mx) for l in ls]
        inv = 1.0 / functools.reduce(lambda a, b: a + b, es)
        on_ref[pl.ds(r, DB, stride=d_max), :] = functools.reduce(
            lambda a, b: a + b, [(e * inv) * x for e, x in zip(es, xs)])

    o_ref[0] = on_ref[...].astype(BF)


def _dil_call(qkv3):
    n_pairs = N_HEADS // 2
    n_sb = n_pairs // 2
    n_pat = len(DIL_PATTERNS)
    return pl.pallas_call(
        _dil_kernel, grid=(BATCH, n_pairs - n_sb),
        in_specs=[pl.BlockSpec((1, SEQ, LANES), lambda b, p: (b, 0, n_sb + p)),
                  pl.BlockSpec((1, SEQ, LANES), lambda b, p: (b, 0, n_pairs + n_sb + p)),
                  pl.BlockSpec((1, SEQ, LANES), lambda b, p: (b, 0, 2 * n_pairs + n_sb + p))],
        out_specs=pl.BlockSpec((1, SEQ, LANES), lambda b, p: (b, 0, p)),
        out_shape=jax.ShapeDtypeStruct((BATCH, SEQ, D_ATTN // 2), BF),
        scratch_shapes=[pltpu.VMEM((3, SEQ, LANES), F32),
                        pltpu.VMEM((n_pat, SEQ, LANES), BF),
                        pltpu.VMEM((n_pat, SEQ, LANES), BF),
                        pltpu.VMEM((n_pat, SEQ, LANES), BF),
                        pltpu.VMEM((n_pat, SEQ, LANES), F32),
                        pltpu.VMEM((n_pat, SEQ, LANES), F32),
                        pltpu.VMEM((SEQ, LANES), F32)],
        compiler_params=pltpu.CompilerParams(dimension_semantics=("parallel", "arbitrary"),
                                             vmem_limit_bytes=VMEM_LIMIT),
        name="dilated_attn")(qkv3, qkv3, qkv3)


def _rotary_tables():
    half = ROT_DIM // 2
    pos = jnp.arange(SEQ, dtype=F32)
    inv_freq = ROPE_THETA ** (-jnp.arange(half, dtype=F32) * 2.0 / ROT_DIM)
    ang = pos[:, None] * inv_freq[None, :]
    cos, sin = jnp.cos(ang), jnp.sin(ang)
    zeros = jnp.zeros((SEQ, HEAD_DIM - ROT_DIM), F32)
    z8 = jnp.zeros((SEQ, half), F32)
    c_head = jnp.concatenate([cos, cos, zeros + 1.0], axis=1)
    s1_head = jnp.concatenate([-sin, z8, zeros], axis=1)
    s2_head = jnp.concatenate([z8, sin, zeros], axis=1)
    two = lambda t: jnp.concatenate([t, t], axis=1)
    return two(c_head), two(s1_head), two(s2_head)


def kernel(x, norm_mix, w_qkv_even, w_o_even, w_qkvf_odd, b_forget, w_o_odd, norm_ffn,
           w_ffn_in, w_ffn_out, norm_final):
    h = x.reshape(M_TOKENS, D_MODEL)
    cos_t, sin1_t, sin2_t = _rotary_tables()
    n_qkv = 3 * D_ATTN
    w_qkv_even = w_qkv_even.astype(BF)
    for layer in range(DEPTH):
        i = layer // 2
        casts = [(w_o_even if layer % 2 == 0 else w_o_odd, i), (w_ffn_in, layer), (w_ffn_out, layer)]
        if layer % 2 == 0:
            qkv, wo, win, wout = _qkv_call(h, norm_mix[layer], w_qkv_even, layer=i,
                                           rot_tables=(cos_t, sin1_t, sin2_t), casts=casts)
            qkv3 = qkv.reshape(BATCH, SEQ, n_qkv)
            oa = _sb_call(qkv3).reshape(M_TOKENS, D_ATTN // 2)
            ob = _dil_call(qkv3).reshape(M_TOKENS, D_ATTN // 2)
            ob_col = 0
        else:
            w = w_qkvf_odd[i]
            wf = jnp.pad(w[:, n_qkv:], ((0, 0), (0, LANES - N_HEADS))).astype(BF)
            qkv, flog, wo, win, wout = _qkv_call(h, norm_mix[layer], w[:, :n_qkv].astype(BF), wf,
                                                 casts=casts)
            bias = jnp.pad(b_forget[i], (0, LANES - N_HEADS)).reshape(1, LANES)
            f_row = _fprep_call(flog.reshape(BATCH, SEQ, LANES), bias)
            o = _fox_call(qkv.reshape(BATCH, SEQ, n_qkv), f_row).reshape(M_TOKENS, D_ATTN)
            oa, ob, ob_col = o, o, 1
        h = _ffn_call(h, oa, ob, ob_col, wo, norm_ffn[layer], win, wout, norm_final,
                      final_norm=(layer == DEPTH - 1))
    return h.reshape(BATCH, SEQ, D_MODEL)
```
